```python
import math
import jax, jax.numpy as jnp
from jax import lax
import numpy as np

D_MODEL = 1024
BATCH = 2
SEQ = 8192
DEPTH = 4
DEC_BATCH = 32
DEC_SEQ = 1
PAST_LEN = 8192
PAGE_SIZE = 128

N_HEADS_A = 4
HEAD_DIM = 64
D_A = N_HEADS_A * 2 * HEAD_DIM
ROPE_THETA = 10000.0
QUERY_BLOCK = 128
D_B = D_MODEL // 2
CONV_WIDTH = 31
D_IN_EVEN = 3 * D_A + 2 * D_B
D_C = D_MODEL
N_GROUPS_C = 4
CHUNK = 128
N_EXPERT_GROUPS = 4
EXPERTS_PER_GROUP = 4
N_EXPERTS = N_EXPERT_GROUPS * EXPERTS_PER_GROUP
TOP_K_FINE = 2
D_FF_EXPERT = 256
N_EVEN = (DEPTH + 1) // 2
N_ODD = DEPTH // 2
EPS = 1e-6
NEG_INF = -1e30

kernel_name = "hybrid_diffattn_conformer_gmlp_hmoe_step"


def rms_norm(x, g):
    xf = x.astype(jnp.float32)
    y = xf * lax.rsqrt(jnp.mean(xf * xf, axis=-1, keepdims=True) + EPS)
    return (y * g.astype(jnp.float32)).astype(x.dtype)


def layer_norm(x, g, b):
    xf = x.astype(jnp.float32)
    mu = jnp.mean(xf, axis=-1, keepdims=True)
    xc = xf - mu
    var = jnp.mean(xc * xc, axis=-1, keepdims=True)
    y = xc * lax.rsqrt(var + EPS) * g.astype(jnp.float32) + b.astype(jnp.float32)
    return y.astype(x.dtype)


def rope(x, pos):
    half = HEAD_DIM // 2
    inv = ROPE_THETA ** (-jnp.arange(half, dtype=jnp.float32) / half)
    ang = pos.astype(jnp.float32)[:, None] * inv[None, :]
    cos = jnp.cos(ang)[:, None, None, :]
    sin = jnp.sin(ang)[:, None, None, :]
    xf = x.astype(jnp.float32)
    x1, x2 = xf[..., :half], xf[..., half:]
    return jnp.concatenate([x1 * cos - x2 * sin, x2 * cos + x1 * sin], axis=-1).astype(x.dtype)


def diff_attend(q, k, v, mask, lam):
    s = jnp.einsum('bqhcd,bkhcd->bhcqk', q.astype(jnp.float32), k.astype(jnp.float32)) / math.sqrt(HEAD_DIM)
    s = jnp.where(mask[None, None, None], s, NEG_INF)
    p = jax.nn.softmax(s, axis=-1)
    a = p[:, :, 0] - lam * p[:, :, 1]
    return jnp.einsum('bhqk,bkhe->bqhe', a, v.astype(jnp.float32))


def prompt_attention(q, k, v, lam):
    B, S = q.shape[0], q.shape[1]
    nb = S // QUERY_BLOCK
    qb = q.reshape(B, nb, QUERY_BLOCK, N_HEADS_A, 2, HEAD_DIM).transpose(1, 0, 2, 3, 4, 5)
    kpos = jnp.arange(S)

    def block(args):
        i, qi = args
        qpos = i * QUERY_BLOCK + jnp.arange(QUERY_BLOCK)
        return diff_attend(qi, k, v, kpos[None, :] <= qpos[:, None], lam)

    out = lax.map(block, (jnp.arange(nb), qb))
    return out.transpose(1, 0, 2, 3, 4).reshape(B, S, N_HEADS_A, 2 * HEAD_DIM)


def decode_attention(q, k, v, past_k, past_v, lam):
    B, S = q.shape[0], q.shape[1]
    P = past_k.shape[1]
    kk = jnp.concatenate([past_k.reshape(B, P, N_HEADS_A, 2, HEAD_DIM), k], axis=1)
    vv = jnp.concatenate([past_v, v], axis=1)
    kpos = jnp.arange(P + S)
    qpos = P + jnp.arange(S)
    return diff_attend(q, kk, vv, kpos[None, :] <= qpos[:, None], lam)


def depthwise_causal_conv(hp, w, b):
    out = lax.conv_general_dilated(hp, w[:, None, :], window_strides=(1,), padding='VALID',
                                   dimension_numbers=('NWC', 'WIO', 'NWC'),
                                   feature_group_count=hp.shape[-1])
    return out + b


def even_mixer(h, pos, past_k, past_v, conv_buf, w_in, w_out, q_g, k_g, lam_p, lam_init,
               o_g, w_dw, b_dw, ln_g, ln_b):
    B, S, _ = h.shape
    proj = h @ w_in
    q, k, v, glu = jnp.split(proj, [D_A, 2 * D_A, 3 * D_A], axis=-1)
    q = rope(rms_norm(q.reshape(B, S, N_HEADS_A, 2, HEAD_DIM), q_g), pos)
    k = rope(rms_norm(k.reshape(B, S, N_HEADS_A, 2, HEAD_DIM), k_g), pos)
    v = v.reshape(B, S, N_HEADS_A, 2 * HEAD_DIM)
    lp = lam_p.astype(jnp.float32)
    lam = jnp.exp(jnp.sum(lp[0] * lp[1])) - jnp.exp(jnp.sum(lp[2] * lp[3])) + lam_init
    if past_k is None:
        o = prompt_attention(q, k, v, lam)
    else:
        o = decode_attention(q, k, v, past_k, past_v, lam)
    o = (rms_norm(o.astype(h.dtype), o_g) * (1.0 - lam_init)).reshape(B, S, D_A)
    a, g = jnp.split(glu, 2, axis=-1)
    hb = a * jax.nn.sigmoid(g)
    if conv_buf is None:
        conv_buf = jnp.zeros((B, CONV_WIDTH - 1, D_B), hb.dtype)
    hp = jnp.concatenate([conv_buf.astype(hb.dtype), hb], axis=1)
    c = jax.nn.silu(layer_norm(depthwise_causal_conv(hp, w_dw, b_dw), ln_g, ln_b))
    y = jnp.concatenate([o, c], axis=-1) @ w_out
    return (y, k.reshape(B, S, N_HEADS_A, 2 * HEAD_DIM), v, hp[:, -(CONV_WIDTH - 1):])


def odd_mixer(h, w_uv, ln_g, ln_b, w_s, b_s, w_o):
    B, S, _ = h.shape
    z = jax.nn.gelu(h @ w_uv, approximate=False)
    u, v = jnp.split(z, 2, axis=-1)
    v = layer_norm(v, ln_g, ln_b)
    L = min(S, CHUNK)
    nc = S // L
    ws = w_s[:, :L, :L] * jnp.tril(jnp.ones((L, L), w_s.dtype))
    vc = v.reshape(B, nc, L, N_GROUPS_C, D_C // N_GROUPS_C)
    sv = jnp.einsum('gts,bcsgd->bctgd', ws, vc) + b_s[:, :L].T[None, None, :, :, None]
    y = (u * sv.reshape(B, S, D_C)) @ w_o
    return y, v


def hier_moe(h, w_rg, b_rg, w_re, b_re, w_up, w_down):
    B, S, D = h.shape
    x = h.reshape(-1, D)
    T = x.shape[0]
    glog = (x @ w_rg).astype(jnp.float32) + b_rg.astype(jnp.float32)
    gprob = jax.nn.softmax(glog, axis=-1)
    gsel = jnp.argmax(glog, axis=-1)
    tok = jnp.arange(T)
    gw = gprob[tok, gsel]
    elog = ((x @ w_re).astype(jnp.float32) + b_re.astype(jnp.float32)).reshape(T, N_EXPERT_GROUPS, EXPERTS_PER_GROUP)
    esel = elog[tok, gsel]
    topv, topi = lax.top_k(esel, TOP_K_FINE)
    ew = jax.nn.softmax(topv, axis=-1) * gw[:, None]
    eid = gsel[:, None] * EXPERTS_PER_GROUP + topi
    gate = jnp.sum(jax.nn.one_hot(eid, N_EXPERTS, dtype=jnp.float32) * ew[..., None], axis=1)
    hu = jnp.einsum('td,edf->tef', x, w_up)
    a, b = jnp.split(hu, 2, axis=-1)
    act = jax.nn.silu(a) * b * gate[..., None].astype(x.dtype)
    y = jnp.einsum('tef,efd->td', act, w_down)
    return y.reshape(B, S, D)


def setup_inputs(seed: int = 0) -> dict:
    key = jax.random.key(seed)
    ks = jax.random.split(key, 32)
    n_pages = PAST_LEN // PAGE_SIZE
    n_pool = (DEC_BATCH * n_pages * 5) // 4
    f32 = jnp.float32

    def nrm(k, shape, scale):
        return jax.random.normal(k, shape, f32) * scale

    page_table = jax.random.permutation(ks[5], n_pool)[:DEC_BATCH * n_pages].reshape(DEC_BATCH, n_pages).astype(jnp.int32)
    return {
        "x_prompt": nrm(ks[0], (BATCH, SEQ, D_MODEL), 1.0),
        "x_sample": nrm(ks[1], (DEC_BATCH, DEC_SEQ, D_MODEL), 1.0),
        "cache_k": nrm(ks[2], (N_EVEN, n_pool, PAGE_SIZE, N_HEADS_A, 2 * HEAD_DIM), 1.0),
        "cache_v": nrm(ks[3], (N_EVEN, n_pool, PAGE_SIZE, N_HEADS_A, 2 * HEAD_DIM), 1.0),
        "state_conv": nrm(ks[4], (N_EVEN, DEC_BATCH, CONV_WIDTH - 1, D_B), 1.0),
        "page_table": page_table,
        "norm_mix": 1.0 + nrm(ks[6], (DEPTH, D_MODEL), 0.02),
        "norm_ffn": 1.0 + nrm(ks[7], (DEPTH, D_MODEL), 0.02),
        "w_in_even": nrm(ks[8], (N_EVEN, D_MODEL, D_IN_EVEN), D_MODEL ** -0.5),
        "w_out_even": nrm(ks[9], (N_EVEN, D_A + D_B, D_MODEL), (D_A + D_B) ** -0.5),
        "q_norm": 1.0 + nrm(ks[10], (N_EVEN, HEAD_DIM), 0.02),
        "k_norm": 1.0 + nrm(ks[11], (N_EVEN, HEAD_DIM), 0.02),
        "lambda_qk": nrm(ks[12], (N_EVEN, 4, HEAD_DIM), 0.1),
        "head_norm": 1.0 + nrm(ks[13], (N_EVEN, 2 * HEAD_DIM), 0.02),
        "w_dw": nrm(ks[14], (N_EVEN, CONV_WIDTH, D_B), CONV_WIDTH ** -0.5),
        "b_dw": nrm(ks[15], (N_EVEN, D_B), 0.02),
        "conv_ln_g": 1.0 + nrm(ks[16], (N_EVEN, D_B), 0.02),
        "conv_ln_b": nrm(ks[17], (N_EVEN, D_B), 0.02),
        "w_uv": nrm(ks[18], (N_ODD, D_MODEL, 2 * D_C), D_MODEL ** -0.5),
        "v_ln_g": 1.0 + nrm(ks[19], (N_ODD, D_C), 0.02),
        "v_ln_b": nrm(ks[20], (N_ODD, D_C), 0.02),
        "w_spatial": nrm(ks[21], (N_ODD, N_GROUPS_C, CHUNK, CHUNK), CHUNK ** -0.5),
        "b_spatial": 1.0 + nrm(ks[22], (N_ODD, N_GROUPS_C, CHUNK), 0.02),
        "w_out_odd": nrm(ks[23], (N_ODD, D_C, D_MODEL), D_C ** -0.5),
        "w_router_group": nrm(ks[24], (DEPTH, D_MODEL, N_EXPERT_GROUPS), D_MODEL ** -0.5),
        "b_router_group": nrm(ks[25], (DEPTH, N_EXPERT_GROUPS), 0.01),
        "w_router_expert": nrm(ks[26], (DEPTH, D_MODEL, N_EXPERTS), D_MODEL ** -0.5),
        "b_router_expert": nrm(ks[27], (DEPTH, N_EXPERTS), 0.01),
        "w_expert_up": nrm(ks[28], (DEPTH, N_EXPERTS, D_MODEL, 2 * D_FF_EXPERT), D_MODEL ** -0.5),
        "w_expert_down": nrm(ks[29], (DEPTH, N_EXPERTS, D_FF_EXPERT, D_MODEL), D_FF_EXPERT ** -0.5),
    }


def reference(x_prompt, x_sample, cache_k, cache_v, state_conv, page_table,
              norm_mix, norm_ffn, w_in_even, w_out_even, q_norm, k_norm, lambda_qk, head_norm,
              w_dw, b_dw, conv_ln_g, conv_ln_b, w_uv, v_ln_g, v_ln_b, w_spatial, b_spatial, w_out_odd,
              w_router_group, b_router_group, w_router_expert, b_router_expert, w_expert_up, w_expert_down):
    db = x_sample.shape[0]
    n_pages = page_table.shape[1]
    past_len = n_pages * cache_k.shape[2]
    pos_p = jnp.arange(x_prompt.shape[1])
    pos_s = past_len + jnp.arange(x_sample.shape[1])
    xp, xs = x_prompt, x_sample
    kp_l, vp_l, ks_l, vs_l, cp_l, cs_l, chs_l = [], [], [], [], [], [], []
    for layer in range(DEPTH):
        i = layer // 2
        hp = rms_norm(xp, norm_mix[layer])
        hs = rms_norm(xs, norm_mix[layer])
        if layer % 2 == 0:
            lam_init = 0.8 - 0.6 * math.exp(-0.3 * layer)
            prm = (w_in_even[i], w_out_even[i], q_norm[i], k_norm[i], lambda_qk[i], lam_init,
                   head_norm[i], w_dw[i], b_dw[i], conv_ln_g[i], conv_ln_b[i])
            yp, kp, vp, cp = even_mixer(hp, pos_p, None, None, None, *prm)
            past_k = cache_k[i][page_table].reshape(db, past_len, N_HEADS_A, 2 * HEAD_DIM)
            past_v = cache_v[i][page_table].reshape(db, past_len, N_HEADS_A, 2 * HEAD_DIM)
            ys, ks, vs, cs = even_mixer(hs, pos_s, past_k, past_v, state_conv[i], *prm)
            kp_l.append(kp); vp_l.append(vp); ks_l.append(ks); vs_l.append(vs)
            cp_l.append(cp); cs_l.append(cs)
        else:
            prm = (w_uv[i], v_ln_g[i], v_ln_b[i], w_spatial[i], b_spatial[i], w_out_odd[i])
            yp, _ = odd_mixer(hp, *prm)
            ys, vchunk = odd_mixer(hs, *prm)
            chs_l.append(vchunk)
        xp = xp + yp
        xs = xs + ys
        moe = (w_router_group[layer], b_router_group[layer], w_router_expert[layer],
               b_router_expert[layer], w_expert_up[layer], w_expert_down[layer])
        xp = xp + hier_moe(rms_norm(xp, norm_ffn[layer]), *moe)
        xs = xs + hier_moe(rms_norm(xs, norm_ffn[layer]), *moe)
    new_k_prompt = jnp.stack(kp_l)
    new_v_prompt = jnp.stack(vp_l)
    new_k_sample = jnp.stack(ks_l)
    new_v_sample = jnp.stack(vs_l)
    new_conv_prompt = jnp.stack(cp_l)
    new_conv_sample = jnp.stack(cs_l)
    new_chunk_v_sample = jnp.stack(chs_l)
    return (xp, xs, new_k_prompt, new_v_prompt, new_k_sample, new_v_sample,
            new_conv_prompt, new_conv_sample, new_chunk_v_sample)
```

```python
import functools
import math

import jax
import jax.numpy as jnp
from jax import lax
from jax.experimental import pallas as pl
from jax.experimental.pallas import tpu as pltpu

F32 = jnp.float32
BF16 = jnp.bfloat16

D_MODEL = 1024
DEPTH = 4
N_HEADS_A = 4
HEAD_DIM = 64
HEAD_V = 2 * HEAD_DIM
D_A = N_HEADS_A * HEAD_V
D_B = D_MODEL // 2
CONV_WIDTH = 31
CONV_HALO = 32
D_IN_EVEN = 3 * D_A + 2 * D_B
D_C = D_MODEL
N_GROUPS_C = 4
CHUNK = 128
N_EXPERT_GROUPS = 4
EXPERTS_PER_GROUP = 4
N_EXPERTS = N_EXPERT_GROUPS * EXPERTS_PER_GROUP
D_FF_EXPERT = 256
ROPE_THETA = 10000.0
EPS = 1e-6
NEG_BIG = -1e30
ROUTER_LANES = 128

VMEM_LIMIT = 52 * 1024 * 1024

TILE_TOKENS = 512
ATTN_BLOCK = 512
CONV_ROWS = 64
PAGES_PER_STEP = 8
HP_SAMPLE_LAYERS = 2


def _params(sem):
    return pltpu.CompilerParams(dimension_semantics=sem, vmem_limit_bytes=VMEM_LIMIT)


def _rms(x, g):
    return x * lax.rsqrt(jnp.mean(x * x, axis=-1, keepdims=True) + EPS) * g


def _layer_norm(x, g, b):
    mu = jnp.mean(x, axis=-1, keepdims=True)
    xc = x - mu
    var = jnp.mean(xc * xc, axis=-1, keepdims=True)
    return xc * lax.rsqrt(var + EPS) * g + b


def _dot(a, b):
    return jnp.dot(a, b, preferred_element_type=F32)


def _dot_nt(a, b):
    return lax.dot_general(a, b, (((1,), (1,)), ((), ())), preferred_element_type=F32)


def _split(x):
    hi = x.astype(BF16)
    return hi, (x - hi.astype(F32)).astype(BF16)


def _stack_split(x):
    hi = x.astype(BF16).astype(F32)
    return jnp.concatenate([hi, x - hi], axis=0).astype(BF16)


def _mm(a, w, hp):
    if not hp:
        return _dot(a.astype(BF16), w)
    m = a.shape[0]
    wh, wl = _split(w)
    top = _dot(_stack_split(a), wh)
    return top[0:m] + top[m:2 * m] + _dot(a.astype(BF16), wl)


def _lambda(lp, lam_init):
    a = jnp.sum(lp[0:1] * lp[1:2], axis=1, keepdims=True)
    b = jnp.sum(lp[2:3] * lp[3:4], axis=1, keepdims=True)
    return jnp.exp(a) - jnp.exp(b) + lam_init


def _even_in_kernel(x_ref, g_ref, w_ref, qg_ref, kg_ref, gmat_ref, cos_ref, sin_ref,
                    q_ref, k32_ref, kb_ref, v32_ref, vb_ref, hb_ref, *, hp):
    h = _rms(x_ref[...], g_ref[...])
    proj = _mm(h, w_ref[...], hp)
    cos = jnp.concatenate([cos_ref[...]] * (D_A // 128), axis=1)
    sin = jnp.concatenate([sin_ref[...]] * (D_A // 128), axis=1)
    lane = lax.broadcasted_iota(jnp.int32, cos.shape, 1)
    first_half = (lane % HEAD_DIM) < (HEAD_DIM // 2)

    def norm_rope(z, gain):
        ms = _mm(z * z, gmat_ref[...], hp)
        zn = z * lax.rsqrt(ms + EPS) * gain
        rot = jnp.where(first_half,
                        pltpu.roll(zn, D_A - HEAD_DIM // 2, 1),
                        pltpu.roll(zn, HEAD_DIM // 2, 1))
        return zn * cos + rot * sin

    q = norm_rope(proj[:, 0:D_A], qg_ref[...])
    k = norm_rope(proj[:, D_A:2 * D_A], kg_ref[...])
    v = proj[:, 2 * D_A:3 * D_A]
    a = proj[:, 3 * D_A:3 * D_A + D_B]
    g = proj[:, 3 * D_A + D_B:]
    q_ref[...] = (q * (1.0 / math.sqrt(HEAD_DIM))).astype(q_ref.dtype)
    k32_ref[...] = k
    kb_ref[...] = k.astype(BF16)
    v32_ref[...] = v
    vb_ref[...] = v.astype(BF16)
    hb_ref[...] = a * jax.nn.sigmoid(g)


def _even_in(x, g, w_in, qg, kg, gmat, cos, sin, tm, pos_blocks, hp=False):
    T = x.shape[0]
    row = lambda i: (i, 0)
    fix = lambda i: (0, 0)
    pos = lambda i: (i % pos_blocks, 0)
    outs = [jax.ShapeDtypeStruct((T, D_A), F32 if hp else BF16),
            jax.ShapeDtypeStruct((T, D_A), F32),
            jax.ShapeDtypeStruct((T, D_A), BF16),
            jax.ShapeDtypeStruct((T, D_A), F32),
            jax.ShapeDtypeStruct((T, D_A), BF16),
            jax.ShapeDtypeStruct((T, D_B), F32)]
    return pl.pallas_call(
        functools.partial(_even_in_kernel, hp=hp),
        grid=(T // tm,),
        in_specs=[pl.BlockSpec((tm, D_MODEL), row),
                  pl.BlockSpec((1, D_MODEL), fix),
                  pl.BlockSpec((D_MODEL, D_IN_EVEN), fix),
                  pl.BlockSpec((1, D_A), fix),
                  pl.BlockSpec((1, D_A), fix),
                  pl.BlockSpec((D_A, D_A), fix),
                  pl.BlockSpec((tm, 128), pos),
                  pl.BlockSpec((tm, 128), pos)],
        out_specs=[pl.BlockSpec((tm, D_A), row)] * 5 + [pl.BlockSpec((tm, D_B), row)],
        out_shape=outs,
        compiler_params=_params(("parallel",)),
        name="even_in",
    )(x, g, w_in, qg, kg, gmat, cos, sin)


def _attn_kernel(q_ref, k_ref, v_ref, lam_ref, og_ref, o_ref,
                 m1_ref, l1_ref, a1_ref, m2_ref, l2_ref, a2_ref, *, blk, lam_init):
    qi = pl.program_id(2)
    q = q_ref[...]
    lane = lax.broadcasted_iota(jnp.int32, q.shape, 1)
    zero = jnp.zeros_like(q)
    q1 = jnp.where(lane < HEAD_DIM, q, zero)
    q2 = jnp.where(lane >= HEAD_DIM, q, zero)

    m1_ref[...] = jnp.full(m1_ref.shape, NEG_BIG, F32)
    m2_ref[...] = jnp.full(m2_ref.shape, NEG_BIG, F32)
    l1_ref[...] = jnp.zeros(l1_ref.shape, F32)
    l2_ref[...] = jnp.zeros(l2_ref.shape, F32)
    a1_ref[...] = jnp.zeros(a1_ref.shape, F32)
    a2_ref[...] = jnp.zeros(a2_ref.shape, F32)

    def update(qc, kblk, vblk, mask, m_ref, l_ref, a_ref):
        s = _dot_nt(qc, kblk)
        if mask is not None:
            s = jnp.where(mask, s, NEG_BIG)
        m_old = m_ref[...]
        m_new = jnp.maximum(m_old, jnp.max(s, axis=1, keepdims=True))
        alpha = jnp.exp(m_old - m_new)
        p = jnp.exp(s - m_new)
        l_ref[...] = alpha * l_ref[...] + jnp.sum(p, axis=1, keepdims=True)
        a_ref[...] = alpha * a_ref[...] + _dot(p.astype(BF16), vblk)
        m_ref[...] = m_new

    def block(start, mask):
        kblk = k_ref[pl.ds(start, blk), :]
        vblk = v_ref[pl.ds(start, blk), :]
        update(q1, kblk, vblk, mask, m1_ref, l1_ref, a1_ref)
        update(q2, kblk, vblk, mask, m2_ref, l2_ref, a2_ref)

    def body(j, carry):
        block(pl.multiple_of(j * blk, blk), None)
        return carry

    lax.fori_loop(0, qi, body, 0)
    r = lax.broadcasted_iota(jnp.int32, (blk, blk), 0)
    c = lax.broadcasted_iota(jnp.int32, (blk, blk), 1)
    block(pl.multiple_of(qi * blk, blk), c <= r)

    lam = _lambda(lam_ref[...], lam_init)
    o = a1_ref[...] / l1_ref[...] - lam * (a2_ref[...] / l2_ref[...])
    o_ref[...] = (_rms(o, og_ref[...]) * (1.0 - lam_init)).astype(o_ref.dtype)


def _prompt_attention(q, kb, vb, lam_p, og, batch, seq, lam_init):
    blk = ATTN_BLOCK
    nq = seq // blk
    kernel = functools.partial(_attn_kernel, blk=blk, lam_init=lam_init)
    scr = []
    for _ in range(2):
        scr += [pltpu.VMEM((blk, 1), F32), pltpu.VMEM((blk, 1), F32),
                pltpu.VMEM((blk, HEAD_V), F32)]
    return pl.pallas_call(
        kernel,
        grid=(batch, N_HEADS_A, nq),
        in_specs=[pl.BlockSpec((blk, HEAD_V), lambda b, h, i: (b * nq + i, h)),
                  pl.BlockSpec((seq, HEAD_V), lambda b, h, i: (b, h)),
                  pl.BlockSpec((seq, HEAD_V), lambda b, h, i: (b, h)),
                  pl.BlockSpec((4, HEAD_DIM), lambda b, h, i: (0, 0)),
                  pl.BlockSpec((1, HEAD_V), lambda b, h, i: (0, 0))],
        out_specs=pl.BlockSpec((blk, HEAD_V), lambda b, h, i: (b * nq + i, h)),
        out_shape=jax.ShapeDtypeStruct((batch * seq, D_A), BF16),
        scratch_shapes=scr,
        compiler_params=_params(("parallel", "parallel", "arbitrary")),
        name="prompt_attn",
    )(q, kb, vb, lam_p, og)


def _decode_kernel(pt_ref, q_ref, kn_ref, vn_ref, lam_ref, og_ref, *rest, pps, lam_init, hp):
    k_refs = rest[:pps]
    v_refs = rest[pps:2 * pps]
    o_ref = rest[2 * pps]
    m_ref, l_ref, acc_ref = rest[2 * pps + 1:]
    s_id = pl.program_id(1)
    rows = 2 * N_HEADS_A

    row = lax.broadcasted_iota(jnp.int32, (rows, D_A), 0)
    lane = lax.broadcasted_iota(jnp.int32, (rows, D_A), 1)
    qf = q_ref[...].astype(F32)
    qm = jnp.where(lane // HEAD_DIM == row, jnp.broadcast_to(qf, (rows, D_A)), 0.0)
    if hp:
        qh, ql = _split(qm)
        q_used = qm
        q_lhs = _stack_split(qm)
    else:
        qh = qm.astype(BF16)
        q_used = qh.astype(F32)

    @pl.when(s_id == 0)
    def _():
        m_ref[...] = jnp.full(m_ref.shape, NEG_BIG, F32)
        l_ref[...] = jnp.zeros(l_ref.shape, F32)
        acc_ref[...] = jnp.zeros(acc_ref.shape, F32)

    def scores(k):
        if not hp:
            return _dot_nt(qh, k.astype(BF16))
        kh, kl = _split(k)
        top = _dot_nt(q_lhs, kh)
        return top[0:rows] + top[rows:2 * rows] + _dot_nt(qh, kl)

    s = jnp.concatenate([scores(k_refs[j][...]) for j in range(pps)], axis=1)
    m_old = m_ref[...]
    m_new = jnp.maximum(m_old, jnp.max(s, axis=1, keepdims=True))
    alpha = jnp.exp(m_old - m_new)
    p = jnp.exp(s - m_new)
    l_ref[...] = alpha * l_ref[...] + jnp.sum(p, axis=1, keepdims=True)
    page = s.shape[1] // pps
    if hp:
        ph, pl_ = _split(p)
        p_lhs = _stack_split(p)
    else:
        ph = p.astype(BF16)
    pv = jnp.zeros((rows, D_A), F32)
    for j in range(pps):
        cols = slice(j * page, (j + 1) * page)
        v = v_refs[j][...]
        if hp:
            vh, vl = _split(v)
            top = _dot(p_lhs[:, cols], vh)
            pv = pv + top[0:rows] + top[rows:2 * rows] + _dot(ph[:, cols], vl)
        else:
            pv = pv + _dot(ph[:, cols], v.astype(BF16))
    acc_ref[...] = alpha * acc_ref[...] + pv
    m_ref[...] = m_new

    @pl.when(s_id == pl.num_programs(1) - 1)
    def _():
        s_new = jnp.sum(q_used * kn_ref[...], axis=1, keepdims=True)
        m_old = m_ref[...]
        m_fin = jnp.maximum(m_old, s_new)
        alpha = jnp.exp(m_old - m_fin)
        p_new = jnp.exp(s_new - m_fin)
        l_fin = alpha * l_ref[...] + p_new
        acc = alpha * acc_ref[...] + p_new * vn_ref[...]
        lam = _lambda(lam_ref[...], lam_init)
        coef = jnp.where(row[:, 0:1] % 2 == 0, 1.0, -lam) / l_fin
        own = lane // HEAD_V == row // 2
        o = jnp.sum(jnp.where(own, acc * coef, 0.0), axis=0, keepdims=True)
        og = og_ref[...]
        heads = [_rms(o[:, h * HEAD_V:(h + 1) * HEAD_V], og) for h in range(N_HEADS_A)]
        o_ref[...] = (jnp.concatenate(heads, axis=1) * (1.0 - lam_init)).astype(o_ref.dtype)


def _decode_attention(page_table, q, k_new, v_new, lam_p, og, cache_k, cache_v, layer_slot,
                      lam_init, hp):
    db, n_pages = page_table.shape
    pps = PAGES_PER_STEP
    page = cache_k.shape[2]
    rows = 2 * N_HEADS_A
    kernel = functools.partial(_decode_kernel, pps=pps, lam_init=lam_init, hp=hp)
    tok = pl.BlockSpec((None, 1, D_A), lambda b, s, pt: (b, 0, 0))

    def page_spec(j):
        return pl.BlockSpec((None, None, page, D_A),
                            lambda b, s, pt: (layer_slot, pt[b * n_pages + s * pps + j], 0, 0))

    grid_spec = pltpu.PrefetchScalarGridSpec(
        num_scalar_prefetch=1,
        grid=(db, n_pages // pps),
        in_specs=[tok, tok, tok,
                  pl.BlockSpec((4, HEAD_DIM), lambda b, s, pt: (0, 0)),
                  pl.BlockSpec((1, HEAD_V), lambda b, s, pt: (0, 0))]
                 + [page_spec(j) for j in range(pps)] * 2,
        out_specs=tok,
        scratch_shapes=[pltpu.VMEM((rows, 1), F32), pltpu.VMEM((rows, 1), F32),
                        pltpu.VMEM((rows, D_A), F32)],
    )
    return pl.pallas_call(
        kernel,
        grid_spec=grid_spec,
        out_shape=jax.ShapeDtypeStruct((db, 1, D_A), F32),
        compiler_params=_params(("parallel", "arbitrary")),
        name="decode_attn",
    )(page_table.reshape(-1), q.reshape(db, 1, D_A), k_new.reshape(db, 1, D_A),
      v_new.reshape(db, 1, D_A), lam_p, og, *([cache_k] * pps), *([cache_v] * pps))


def _conv_tail(conv, lng, lnb, o, x, w_ref, hp=False):
    y = _layer_norm(conv, lng, lnb)
    c = y * jax.nn.sigmoid(y)
    return (x + _mm(o, w_ref[0:D_A, :], hp) + _mm(c, w_ref[D_A:D_A + D_B, :], hp))


def _even_out_kernel(hb_ref, hprev_ref, o_ref, x_ref, wdw_ref, bdw_ref, lng_ref, lnb_ref,
                     w_ref, y_ref, win_ref, conv_ref, *, tm, tiles_per_seq):
    i = pl.program_id(0)
    first = (i % tiles_per_seq) == 0
    win_ref[0:CONV_HALO, :] = jnp.where(first, 0.0, hprev_ref[...])
    win_ref[CONV_HALO:CONV_HALO + tm, :] = hb_ref[...]
    lead = CONV_HALO - (CONV_WIDTH - 1)
    bias = bdw_ref[...]
    for r0 in range(0, tm, CONV_ROWS):
        acc = jnp.broadcast_to(bias, (CONV_ROWS, D_B))
        for j in range(CONV_WIDTH):
            acc = acc + win_ref[r0 + lead + j:r0 + lead + j + CONV_ROWS, :] * wdw_ref[j:j + 1, :]
        conv_ref[r0:r0 + CONV_ROWS, :] = acc
    y_ref[...] = _conv_tail(conv_ref[...], lng_ref[...], lnb_ref[...], o_ref[...], x_ref[...],
                            w_ref)


def _even_out(hb, o, x, wdw, bdw, lng, lnb, w_out, tm, seq):
    T = x.shape[0]
    tiles_per_seq = seq // tm
    halo_blocks = tm // CONV_HALO
    row = lambda i: (i, 0)
    fix = lambda i: (0, 0)
    kernel = functools.partial(_even_out_kernel, tm=tm, tiles_per_seq=tiles_per_seq)
    return pl.pallas_call(
        kernel,
        grid=(T // tm,),
        in_specs=[pl.BlockSpec((tm, D_B), row),
                  pl.BlockSpec((CONV_HALO, D_B), lambda i: (jnp.maximum(i * halo_blocks - 1, 0), 0)),
                  pl.BlockSpec((tm, D_A), row),
                  pl.BlockSpec((tm, D_MODEL), row),
                  pl.BlockSpec((CONV_HALO, D_B), fix),
                  pl.BlockSpec((1, D_B), fix),
                  pl.BlockSpec((1, D_B), fix),
                  pl.BlockSpec((1, D_B), fix),
                  pl.BlockSpec((D_A + D_B, D_MODEL), fix)],
        out_specs=pl.BlockSpec((tm, D_MODEL), row),
        out_shape=jax.ShapeDtypeStruct((T, D_MODEL), F32),
        scratch_shapes=[pltpu.VMEM((CONV_HALO + tm, D_B), F32), pltpu.VMEM((tm, D_B), F32)],
        compiler_params=_params(("parallel",)),
        name="even_out",
    )(hb, hb, o, x, wdw, bdw, lng, lnb, w_out)


def _even_out_sample_kernel(st_ref, hb_ref, o_ref, x_ref, wdw_ref, bdw_ref, lng_ref, lnb_ref,
                            w_ref, y_ref, *, hp):
    w = wdw_ref[...]
    conv = jnp.sum(st_ref[...] * w[0:CONV_WIDTH - 1, :][None], axis=1)
    conv = conv + hb_ref[...] * w[CONV_WIDTH - 1:CONV_WIDTH, :] + bdw_ref[...]
    y_ref[...] = _conv_tail(conv, lng_ref[...], lnb_ref[...], o_ref[...], x_ref[...], w_ref,
                            hp)


def _even_out_sample(state, hb, o, x, wdw, bdw, lng, lnb, w_out, hp):
    db = x.shape[0]
    return pl.pallas_call(
        functools.partial(_even_out_sample_kernel, hp=hp),
        out_shape=jax.ShapeDtypeStruct((db, D_MODEL), F32),
        compiler_params=pltpu.CompilerParams(vmem_limit_bytes=VMEM_LIMIT),
        name="even_out_sample",
    )(state, hb, o, x, wdw, bdw, lng, lnb, w_out)


def _odd_front(x_ref, g_ref, wuv_ref, lng_ref, lnb_ref, hp=False):
    h = _rms(x_ref[...], g_ref[...])
    z = _mm(h, wuv_ref[...], hp)
    z = 0.5 * z * (1.0 + lax.erf(z * (1.0 / math.sqrt(2.0))))
    u = z[:, 0:D_C]
    v = _layer_norm(z[:, D_C:2 * D_C], lng_ref[...], lnb_ref[...])
    return u, v


def _odd_kernel(x_ref, g_ref, wuv_ref, lng_ref, lnb_ref, ws_ref, bs_ref, wo_ref, y_ref,
                sv_ref, *, tm):
    u, v = _odd_front(x_ref, g_ref, wuv_ref, lng_ref, lnb_ref)
    vb = v.astype(BF16)
    r = lax.broadcasted_iota(jnp.int32, (CHUNK, CHUNK), 0)
    c = lax.broadcasted_iota(jnp.int32, (CHUNK, CHUNK), 1)
    gw = D_C // N_GROUPS_C
    for g in range(N_GROUPS_C):
        wsg = jnp.where(c <= r, ws_ref[g], 0.0).astype(BF16)
        for ch in range(tm // CHUNK):
            rows = slice(ch * CHUNK, (ch + 1) * CHUNK)
            cols = slice(g * gw, (g + 1) * gw)
            sv_ref[rows, cols] = _dot(wsg, vb[rows, cols]) + bs_ref[:, cols]
    y_ref[...] = x_ref[...] + _dot((u * sv_ref[...]).astype(BF16), wo_ref[...])


def _odd(x, g, wuv, lng, lnb, ws, bs, wo, tm):
    T = x.shape[0]
    row = lambda i: (i, 0)
    fix = lambda i: (0, 0)
    return pl.pallas_call(
        functools.partial(_odd_kernel, tm=tm),
        grid=(T // tm,),
        in_specs=[pl.BlockSpec((tm, D_MODEL), row),
                  pl.BlockSpec((1, D_MODEL), fix),
                  pl.BlockSpec((D_MODEL, 2 * D_C), fix),
                  pl.BlockSpec((1, D_C), fix),
                  pl.BlockSpec((1, D_C), fix),
                  pl.BlockSpec((N_GROUPS_C, CHUNK, CHUNK), lambda i: (0, 0, 0)),
                  pl.BlockSpec((CHUNK, D_C), fix),
                  pl.BlockSpec((D_C, D_MODEL), fix)],
        out_specs=pl.BlockSpec((tm, D_MODEL), row),
        out_shape=jax.ShapeDtypeStruct((T, D_MODEL), F32),
        scratch_shapes=[pltpu.VMEM((tm, D_C), F32)],
        compiler_params=_params(("parallel",)),
        name="odd_mixer",
    )(x, g, wuv, lng, lnb, ws, bs, wo)


def _odd_sample_kernel(x_ref, g_ref, wuv_ref, lng_ref, lnb_ref, ws0_ref, bs0_ref, wo_ref,
                       y_ref, v_ref, *, hp):
    u, v = _odd_front(x_ref, g_ref, wuv_ref, lng_ref, lnb_ref, hp)
    v_ref[...] = v
    sv = v * ws0_ref[...] + bs0_ref[...]
    y_ref[...] = x_ref[...] + _mm(u * sv, wo_ref[...], hp)


def _odd_sample(x, g, wuv, lng, lnb, ws0, bs0, wo, hp):
    db = x.shape[0]
    return pl.pallas_call(
        functools.partial(_odd_sample_kernel, hp=hp),
        out_shape=[jax.ShapeDtypeStruct((db, D_MODEL), F32),
                   jax.ShapeDtypeStruct((db, D_C), F32)],
        compiler_params=pltpu.CompilerParams(vmem_limit_bytes=VMEM_LIMIT),
        name="odd_mixer_sample",
    )(x, g, wuv, lng, lnb, ws0, bs0, wo)


def _route(logits):
    lane = lax.broadcasted_iota(jnp.int32, logits.shape, 1)
    lane_f = lane.astype(F32)
    none = float(ROUTER_LANES)
    is_g = lane < N_EXPERT_GROUPS
    gl = jnp.where(is_g, logits, NEG_BIG)
    gmax = jnp.max(gl, axis=1, keepdims=True)
    gidx = jnp.min(jnp.where(is_g & (gl == gmax), lane_f, none), axis=1, keepdims=True)
    gw = 1.0 / jnp.sum(jnp.where(is_g, jnp.exp(gl - gmax), 0.0), axis=1, keepdims=True)
    e_f = lane_f - float(N_EXPERT_GROUPS)
    is_e = (lane >= N_EXPERT_GROUPS) & (lane < N_EXPERT_GROUPS + N_EXPERTS)
    grp_lo = gidx * float(EXPERTS_PER_GROUP)
    in_grp = is_e & (e_f >= grp_lo) & (e_f < grp_lo + float(EXPERTS_PER_GROUP))
    cand = jnp.where(in_grp, logits, NEG_BIG)
    v1 = jnp.max(cand, axis=1, keepdims=True)
    i1 = jnp.min(jnp.where(in_grp & (cand == v1), lane_f, none), axis=1, keepdims=True)
    rest = in_grp & (lane_f != i1)
    cand2 = jnp.where(rest, logits, NEG_BIG)
    v2 = jnp.max(cand2, axis=1, keepdims=True)
    i2 = jnp.min(jnp.where(rest & (cand2 == v2), lane_f, none), axis=1, keepdims=True)
    t = jnp.exp(v2 - v1)
    p1 = 1.0 / (1.0 + t)
    p2 = t * p1
    return jnp.where(lane_f == i1, p1 * gw, jnp.where(lane_f == i2, p2 * gw, 0.0))


def _moe_kernel(x_ref, g_ref, wr_ref, br_ref, wup_ref, wdn_ref, y_ref, xn_ref, gate_ref,
                acc_ref, *, hp):
    grp = pl.program_id(1)

    @pl.when(grp == 0)
    def _():
        xn = _rms(x_ref[...], g_ref[...]).astype(xn_ref.dtype)
        xn_ref[...] = xn
        gate_ref[...] = _route(_mm(xn, wr_ref[...], hp) + br_ref[...])
        acc_ref[...] = jnp.zeros(acc_ref.shape, F32)

    xn = xn_ref[...]
    gate = gate_ref[...]
    lane = lax.broadcasted_iota(jnp.int32, gate.shape, 1)
    acc = acc_ref[...]
    for j in range(EXPERTS_PER_GROUP):
        e_lane = N_EXPERT_GROUPS + grp * EXPERTS_PER_GROUP + j
        ge = jnp.sum(jnp.where(lane == e_lane, gate, 0.0), axis=1, keepdims=True)
        hu = _mm(xn, wup_ref[j], hp)
        a = hu[:, 0:D_FF_EXPERT]
        b = hu[:, D_FF_EXPERT:]
        act = a * jax.nn.sigmoid(a) * b * ge
        acc = acc + _mm(act, wdn_ref[j], hp)
    acc_ref[...] = acc

    @pl.when(grp == pl.num_programs(1) - 1)
    def _():
        y_ref[...] = x_ref[...] + acc_ref[...]


def _moe(x, g, wr, br, wup, wdn, tm, hp=False):
    T = x.shape[0]
    row = lambda i, e: (i, 0)
    fix = lambda i, e: (0, 0)
    return pl.pallas_call(
        functools.partial(_moe_kernel, hp=hp),
        grid=(T // tm, N_EXPERT_GROUPS),
        in_specs=[pl.BlockSpec((tm, D_MODEL), row),
                  pl.BlockSpec((1, D_MODEL), fix),
                  pl.BlockSpec((D_MODEL, ROUTER_LANES), fix),
                  pl.BlockSpec((1, ROUTER_LANES), fix),
                  pl.BlockSpec((EXPERTS_PER_GROUP, D_MODEL, 2 * D_FF_EXPERT),
                               lambda i, e: (e, 0, 0)),
                  pl.BlockSpec((EXPERTS_PER_GROUP, D_FF_EXPERT, D_MODEL),
                               lambda i, e: (e, 0, 0))],
        out_specs=pl.BlockSpec((tm, D_MODEL), row),
        out_shape=jax.ShapeDtypeStruct((T, D_MODEL), F32),
        scratch_shapes=[pltpu.VMEM((tm, D_MODEL), F32 if hp else BF16),
                        pltpu.VMEM((tm, ROUTER_LANES), F32),
                        pltpu.VMEM((tm, D_MODEL), F32)],
        compiler_params=_params(("parallel", "arbitrary")),
        name="hier_moe",
    )(x, g, wr, br, wup, wdn)


def _rope_tables(pos):
    half = HEAD_DIM // 2
    inv = ROPE_THETA ** (-jnp.arange(half, dtype=F32) / half)
    ang = pos.astype(F32)[:, None] * inv[None, :]
    cos, sin = jnp.cos(ang), jnp.sin(ang)
    return (jnp.concatenate([cos, cos, cos, cos], axis=1),
            jnp.concatenate([-sin, sin, -sin, sin], axis=1))


def _tile_gain(g, reps):
    return jnp.tile(g.astype(F32), reps)[None, :]


def kernel(x_prompt, x_sample, cache_k, cache_v, state_conv, page_table, norm_mix, norm_ffn, w_in_even, w_out_even, q_norm, k_norm, lambda_qk, head_norm, w_dw, b_dw, conv_ln_g, conv_ln_b, w_uv, v_ln_g, v_ln_b, w_spatial, b_spatial, w_out_odd, w_router_group, b_router_group, w_router_expert, b_router_expert, w_expert_up, w_expert_down):
    batch, seq, _ = x_prompt.shape
    db = x_sample.shape[0]
    n_pages = page_table.shape[1]
    page = cache_k.shape[2]
    past_len = n_pages * page
    tm = TILE_TOKENS

    xp = x_prompt.reshape(batch * seq, D_MODEL)
    xs = x_sample.reshape(db, D_MODEL)
    cos_p, sin_p = _rope_tables(jnp.arange(seq))
    cos_s, sin_s = _rope_tables(jnp.full((db,), past_len))
    ck = cache_k.reshape(cache_k.shape[0], cache_k.shape[1], page, D_A)
    cv = cache_v.reshape(cache_v.shape[0], cache_v.shape[1], page, D_A)
    group_of = jnp.arange(D_A) // HEAD_DIM
    gmat32 = jnp.where(group_of[:, None] == group_of[None, :], 1.0 / HEAD_DIM, 0.0).astype(F32)
    gmat = gmat32.astype(BF16)

    kp_l, vp_l, ks_l, vs_l, cp_l, cs_l, chs_l = [], [], [], [], [], [], []
    for layer in range(DEPTH):
        i = layer // 2
        g_mix = norm_mix[layer][None, :]
        hp = layer < HP_SAMPLE_LAYERS
        if layer % 2 == 0:
            lam_init = 0.8 - 0.6 * math.exp(-0.3 * layer)
            w_in = w_in_even[i].astype(BF16)
            w_out = w_out_even[i].astype(BF16)
            qg = _tile_gain(q_norm[i], D_A // HEAD_DIM)
            kg = _tile_gain(k_norm[i], D_A // HEAD_DIM)
            og = head_norm[i][None, :]
            wdw = jnp.pad(w_dw[i], ((0, CONV_HALO - CONV_WIDTH), (0, 0)))
            bdw, lng, lnb = b_dw[i][None, :], conv_ln_g[i][None, :], conv_ln_b[i][None, :]
            q, k32, kb, v32, vb, hb = _even_in(xp, g_mix, w_in, qg, kg, gmat, cos_p, sin_p,
                                               tm, seq // tm)
            o = _prompt_attention(q, kb, vb, lambda_qk[i], og, batch, seq, lam_init)
            xp = _even_out(hb, o, xp, wdw, bdw, lng, lnb, w_out, tm, seq)
            kp_l.append(k32.reshape(batch, seq, N_HEADS_A, HEAD_V))
            vp_l.append(v32.reshape(batch, seq, N_HEADS_A, HEAD_V))
            cp_l.append(hb.reshape(batch, seq, D_B)[:, seq - (CONV_WIDTH - 1):, :])
            qs, ks32, _, vs32, _, hbs = _even_in(
                xs, g_mix, w_in_even[i] if hp else w_in, qg, kg, gmat32 if hp else gmat,
                cos_s, sin_s, db, 1, hp)
            os_ = _decode_attention(page_table, qs, ks32, vs32, lambda_qk[i], og, ck, cv, i,
                                    lam_init, hp)
            xs = _even_out_sample(state_conv[i], hbs, os_.reshape(db, D_A), xs, wdw, bdw, lng,
                                  lnb, w_out_even[i] if hp else w_out, hp)
            ks_l.append(ks32.reshape(db, 1, N_HEADS_A, HEAD_V))
            vs_l.append(vs32.reshape(db, 1, N_HEADS_A, HEAD_V))
            cs_l.append(jnp.concatenate([state_conv[i][:, 1:, :], hbs[:, None, :]], axis=1))
        else:
            wuv = w_uv[i].astype(BF16)
            wo = w_out_odd[i].astype(BF16)
            lng, lnb = v_ln_g[i][None, :], v_ln_b[i][None, :]
            gw = D_C // N_GROUPS_C
            bs = jnp.repeat(b_spatial[i].T, gw, axis=1)
            xp = _odd(xp, g_mix, wuv, lng, lnb, w_spatial[i], bs, wo, tm)
            ws0 = jnp.repeat(w_spatial[i][:, 0, 0], gw)[None, :]
            xs, vch = _odd_sample(xs, g_mix, w_uv[i] if hp else wuv, lng, lnb, ws0, bs[0:1, :],
                                  w_out_odd[i] if hp else wo, hp)
            chs_l.append(vch.reshape(db, 1, D_C))
        g_ffn = norm_ffn[layer][None, :]
        pad = ROUTER_LANES - N_EXPERT_GROUPS - N_EXPERTS
        wr32 = jnp.pad(jnp.concatenate([w_router_group[layer], w_router_expert[layer]], axis=1),
                       ((0, 0), (0, pad)))
        wr = wr32.astype(BF16)
        br = jnp.pad(jnp.concatenate([b_router_group[layer], b_router_expert[layer]]),
                     (0, pad))[None, :]
        wup = w_expert_up[layer].astype(BF16)
        wdn = w_expert_down[layer].astype(BF16)
        xp = _moe(xp, g_ffn, wr, br, wup, wdn, tm)
        if hp:
            xs = _moe(xs, g_ffn, wr32, br, w_expert_up[layer], w_expert_down[layer], db, True)
        else:
            xs = _moe(xs, g_ffn, wr, br, wup, wdn, db)

    return (xp.reshape(batch, seq, D_MODEL), xs.reshape(db, 1, D_MODEL),
            jnp.stack(kp_l), jnp.stack(vp_l), jnp.stack(ks_l), jnp.stack(vs_l),
            jnp.stack(cp_l), jnp.stack(cs_l), jnp.stack(chs_l))
```

```python
import functools
import math

import jax
import jax.numpy as jnp
from jax import lax
from jax.experimental import pallas as pl
from jax.experimental.pallas import tpu as pltpu

F32 = jnp.float32
BF16 = jnp.bfloat16

D_MODEL = 1024
DEPTH = 4
N_HEADS_A = 4
HEAD_DIM = 64
HEAD_V = 2 * HEAD_DIM
D_A = N_HEADS_A * HEAD_V
D_B = D_MODEL // 2
CONV_WIDTH = 31
CONV_HALO = 32
D_IN_EVEN = 3 * D_A + 2 * D_B
D_C = D_MODEL
N_GROUPS_C = 4
CHUNK = 128
N_EXPERT_GROUPS = 4
EXPERTS_PER_GROUP = 4
N_EXPERTS = N_EXPERT_GROUPS * EXPERTS_PER_GROUP
D_FF_EXPERT = 256
ROPE_THETA = 10000.0
EPS = 1e-6
NEG_BIG = -1e30
ROUTER_LANES = 128

VMEM_LIMIT = 52 * 1024 * 1024

TILE_TOKENS = 512
ATTN_BLOCK = 512
CONV_ROWS = 64
PAGES_PER_STEP = 16
HP_SAMPLE_LAYERS = 2


def _params(sem):
    return pltpu.CompilerParams(dimension_semantics=sem, vmem_limit_bytes=VMEM_LIMIT)


def _rms(x, g):
    return x * lax.rsqrt(jnp.mean(x * x, axis=-1, keepdims=True) + EPS) * g


def _layer_norm(x, g, b):
    mu = jnp.mean(x, axis=-1, keepdims=True)
    xc = x - mu
    var = jnp.mean(xc * xc, axis=-1, keepdims=True)
    return xc * lax.rsqrt(var + EPS) * g + b


def _dot(a, b):
    return jnp.dot(a, b, preferred_element_type=F32)


def _dot_nt(a, b):
    return lax.dot_general(a, b, (((1,), (1,)), ((), ())), preferred_element_type=F32)


def _split(x):
    hi = x.astype(BF16)
    return hi, (x - hi.astype(F32)).astype(BF16)


def _stack_split(x):
    hi = x.astype(BF16).astype(F32)
    return jnp.concatenate([hi, x - hi], axis=0).astype(BF16)


def _mm(a, w, hp):
    if not hp:
        return _dot(a.astype(BF16), w)
    m = a.shape[0]
    wh, wl = _split(w)
    top = _dot(_stack_split(a), wh)
    return top[0:m] + top[m:2 * m] + _dot(a.astype(BF16), wl)


def _lambda(lp, lam_init):
    a = jnp.sum(lp[0:1] * lp[1:2], axis=1, keepdims=True)
    b = jnp.sum(lp[2:3] * lp[3:4], axis=1, keepdims=True)
    return jnp.exp(a) - jnp.exp(b) + lam_init


def _even_in_kernel(x_ref, g_ref, w_ref, qg_ref, kg_ref, gmat_ref, cos_ref, sin_ref,
                    q_ref, k32_ref, kb_ref, v32_ref, vb_ref, hb_ref, *, hp):
    h = _rms(x_ref[...], g_ref[...])
    proj = _mm(h, w_ref[...], hp)
    cos = jnp.concatenate([cos_ref[...]] * (D_A // 128), axis=1)
    sin = jnp.concatenate([sin_ref[...]] * (D_A // 128), axis=1)
    lane = lax.broadcasted_iota(jnp.int32, cos.shape, 1)
    first_half = (lane % HEAD_DIM) < (HEAD_DIM // 2)

    def norm_rope(z, gain):
        ms = _mm(z * z, gmat_ref[...], hp)
        zn = z * lax.rsqrt(ms + EPS) * gain
        rot = jnp.where(first_half,
                        pltpu.roll(zn, D_A - HEAD_DIM // 2, 1),
                        pltpu.roll(zn, HEAD_DIM // 2, 1))
        return zn * cos + rot * sin

    q = norm_rope(proj[:, 0:D_A], qg_ref[...])
    k = norm_rope(proj[:, D_A:2 * D_A], kg_ref[...])
    v = proj[:, 2 * D_A:3 * D_A]
    a = proj[:, 3 * D_A:3 * D_A + D_B]
    g = proj[:, 3 * D_A + D_B:]
    q_ref[...] = (q * (1.0 / math.sqrt(HEAD_DIM))).astype(q_ref.dtype)
    k32_ref[...] = k
    kb_ref[...] = k.astype(BF16)
    v32_ref[...] = v
    vb_ref[...] = v.astype(BF16)
    hb_ref[...] = a * jax.nn.sigmoid(g)


def _even_in(x, g, w_in, qg, kg, gmat, cos, sin, tm, pos_blocks, hp=False):
    T = x.shape[0]
    row = lambda i: (i, 0)
    fix = lambda i: (0, 0)
    pos = lambda i: (i % pos_blocks, 0)
    outs = [jax.ShapeDtypeStruct((T, D_A), F32 if hp else BF16),
            jax.ShapeDtypeStruct((T, D_A), F32),
            jax.ShapeDtypeStruct((T, D_A), BF16),
            jax.ShapeDtypeStruct((T, D_A), F32),
            jax.ShapeDtypeStruct((T, D_A), BF16),
            jax.ShapeDtypeStruct((T, D_B), F32)]
    return pl.pallas_call(
        functools.partial(_even_in_kernel, hp=hp),
        grid=(T // tm,),
        in_specs=[pl.BlockSpec((tm, D_MODEL), row),
                  pl.BlockSpec((1, D_MODEL), fix),
                  pl.BlockSpec((D_MODEL, D_IN_EVEN), fix),
                  pl.BlockSpec((1, D_A), fix),
                  pl.BlockSpec((1, D_A), fix),
                  pl.BlockSpec((D_A, D_A), fix),
                  pl.BlockSpec((tm, 128), pos),
                  pl.BlockSpec((tm, 128), pos)],
        out_specs=[pl.BlockSpec((tm, D_A), row)] * 5 + [pl.BlockSpec((tm, D_B), row)],
        out_shape=outs,
        compiler_params=_params(("parallel",)),
        name="even_in",
    )(x, g, w_in, qg, kg, gmat, cos, sin)


def _attn_kernel(q_ref, k_ref, v_ref, lam_ref, og_ref, o_ref, v1_ref, m_ref, acc_ref, *,
                 blk, lam_init):
    qi = pl.program_id(2)

    @pl.when(qi == 0)
    def _():
        v1_ref[:, 0:HEAD_V] = v_ref[...]
        v1_ref[:, HEAD_V:2 * HEAD_V] = jnp.ones((v1_ref.shape[0], HEAD_V), BF16)

    q = q_ref[...]
    lane = lax.broadcasted_iota(jnp.int32, q.shape, 1)
    zero = jnp.zeros_like(q)
    qq = jnp.concatenate([jnp.where(lane < HEAD_DIM, q, zero),
                          jnp.where(lane >= HEAD_DIM, q, zero)], axis=0)
    m_ref[...] = jnp.full(m_ref.shape, NEG_BIG, F32)
    acc_ref[...] = jnp.zeros(acc_ref.shape, F32)

    def block(start, mask):
        kblk = k_ref[pl.ds(start, blk), :]
        vblk = v1_ref[pl.ds(start, blk), :]
        s = _dot_nt(qq, kblk)
        if mask is not None:
            s = jnp.where(mask, s, NEG_BIG)
        m_old = m_ref[...]
        m_new = jnp.maximum(m_old, jnp.max(s, axis=1, keepdims=True))
        alpha = jnp.exp(m_old - m_new)
        p = jnp.exp((s - jnp.concatenate([m_new] * (blk // 128), axis=1)).astype(BF16))
        acc_ref[...] = (jnp.concatenate([alpha, alpha], axis=1) * acc_ref[...]
                        + _dot(p, vblk))
        m_ref[...] = m_new

    def body(j, carry):
        block(pl.multiple_of(j * blk, blk), None)
        return carry

    lax.fori_loop(0, qi, body, 0)
    r = lax.broadcasted_iota(jnp.int32, (2 * blk, blk), 0)
    c = lax.broadcasted_iota(jnp.int32, (2 * blk, blk), 1)
    block(pl.multiple_of(qi * blk, blk), c <= jnp.where(r >= blk, r - blk, r))

    lam = _lambda(lam_ref[...], lam_init)
    acc = acc_ref[...]
    ratio = acc[:, 0:HEAD_V] / acc[:, HEAD_V:2 * HEAD_V]
    o = ratio[0:blk] - lam * ratio[blk:2 * blk]
    o_ref[...] = (_rms(o, og_ref[...]) * (1.0 - lam_init)).astype(o_ref.dtype)


def _prompt_attention(q, kb, vb, lam_p, og, batch, seq, lam_init):
    blk = ATTN_BLOCK
    nq = seq // blk
    kernel = functools.partial(_attn_kernel, blk=blk, lam_init=lam_init)
    scr = [pltpu.VMEM((seq, 2 * HEAD_V), BF16), pltpu.VMEM((2 * blk, HEAD_V), F32),
           pltpu.VMEM((2 * blk, 2 * HEAD_V), F32)]
    return pl.pallas_call(
        kernel,
        grid=(batch, N_HEADS_A, nq),
        in_specs=[pl.BlockSpec((blk, HEAD_V), lambda b, h, i: (b * nq + i, h)),
                  pl.BlockSpec((seq, HEAD_V), lambda b, h, i: (b, h)),
                  pl.BlockSpec((seq, HEAD_V), lambda b, h, i: (b, h)),
                  pl.BlockSpec((4, HEAD_DIM), lambda b, h, i: (0, 0)),
                  pl.BlockSpec((1, HEAD_V), lambda b, h, i: (0, 0))],
        out_specs=pl.BlockSpec((blk, HEAD_V), lambda b, h, i: (b * nq + i, h)),
        out_shape=jax.ShapeDtypeStruct((batch * seq, D_A), BF16),
        scratch_shapes=scr,
        compiler_params=_params(("parallel", "parallel", "arbitrary")),
        name="prompt_attn",
    )(q, kb, vb, lam_p, og)


def _decode_kernel(pt_ref, q_ref, kn_ref, vn_ref, lam_ref, og_ref, *rest, pps, lam_init, hp):
    k_refs = rest[:pps]
    v_refs = rest[pps:2 * pps]
    o_ref = rest[2 * pps]
    m_ref, l_ref, acc_ref, r_ref = rest[2 * pps + 1:]
    s_id = pl.program_id(1)
    rows = 2 * N_HEADS_A
    prow = k_refs[0].shape[0]

    q8 = q_ref[...]
    qh = q8.astype(BF16)
    if hp:
        q_used = q8
        q_lhs = _stack_split(q8)
    else:
        q_used = qh.astype(F32)
    row = lax.broadcasted_iota(jnp.int32, (rows, prow), 0)
    col = lax.broadcasted_iota(jnp.int32, (rows, prow), 1)
    own = (col % N_HEADS_A) == (row // 2)

    @pl.when(s_id == 0)
    def _():
        m_ref[...] = jnp.full(m_ref.shape, NEG_BIG, F32)
        l_ref[...] = jnp.zeros(l_ref.shape, F32)
        acc_ref[...] = jnp.zeros(acc_ref.shape, F32)

    def scores(k):
        if not hp:
            s = _dot_nt(qh, k.astype(BF16))
        else:
            kh, kl = _split(k)
            top = _dot_nt(q_lhs, kh)
            s = top[0:rows] + top[rows:2 * rows] + _dot_nt(qh, kl)
        return jnp.where(own, s, NEG_BIG)

    s = jnp.concatenate([scores(k_refs[j][...]) for j in range(pps)], axis=1)
    m_old = m_ref[...]
    m_new = jnp.maximum(m_old, jnp.max(s, axis=1, keepdims=True))
    alpha = jnp.exp(m_old - m_new)
    p = jnp.exp(s - m_new)
    l_ref[...] = alpha * l_ref[...] + jnp.sum(p, axis=1, keepdims=True)
    ph = p.astype(BF16)
    if hp:
        p_lhs = _stack_split(p)
    pv = jnp.zeros((rows, HEAD_V), F32)
    for j in range(pps):
        cols = slice(j * prow, (j + 1) * prow)
        v = v_refs[j][...]
        if hp:
            vh, vl = _split(v)
            top = _dot(p_lhs[:, cols], vh)
            pv = pv + top[0:rows] + top[rows:2 * rows] + _dot(ph[:, cols], vl)
        else:
            pv = pv + _dot(ph[:, cols], v.astype(BF16))
    acc_ref[...] = alpha * acc_ref[...] + pv
    m_ref[...] = m_new

    @pl.when(s_id == pl.num_programs(1) - 1)
    def _():
        s_new = jnp.sum(q_used * kn_ref[...], axis=1, keepdims=True)
        m_old = m_ref[...]
        m_fin = jnp.maximum(m_old, s_new)
        alpha = jnp.exp(m_old - m_fin)
        p_new = jnp.exp(s_new - m_fin)
        l_fin = alpha * l_ref[...] + p_new
        r_ref[...] = (alpha * acc_ref[...] + p_new * vn_ref[...]) / l_fin
        lam = _lambda(lam_ref[...], lam_init)
        o = (r_ref[pl.ds(0, N_HEADS_A, stride=2), :]
             - lam * r_ref[pl.ds(1, N_HEADS_A, stride=2), :])
        o_ref[...] = _rms(o, og_ref[...]) * (1.0 - lam_init)


def _decode_attention(page_table, q, k_new, v_new, lam_p, og, cache_k, cache_v, layer_slot,
                      lam_init, hp):
    db, n_pages = page_table.shape
    pps = PAGES_PER_STEP
    prow = cache_k.shape[2]
    rows = 2 * N_HEADS_A
    half = jnp.arange(HEAD_V) // HEAD_DIM
    map_mask = (half[None, :] == jnp.arange(2)[:, None]).astype(F32)
    q8 = (q.astype(F32).reshape(db, N_HEADS_A, 1, HEAD_V) * map_mask).reshape(db, rows, HEAD_V)
    k8 = jnp.repeat(k_new.reshape(db, N_HEADS_A, HEAD_V), 2, axis=1)
    v8 = jnp.repeat(v_new.reshape(db, N_HEADS_A, HEAD_V), 2, axis=1)
    kernel = functools.partial(_decode_kernel, pps=pps, lam_init=lam_init, hp=hp)
    tok = pl.BlockSpec((None, rows, HEAD_V), lambda b, s, pt: (b, 0, 0))

    def page_spec(j):
        return pl.BlockSpec((None, None, prow, HEAD_V),
                            lambda b, s, pt: (layer_slot, pt[b * n_pages + s * pps + j], 0, 0))

    grid_spec = pltpu.PrefetchScalarGridSpec(
        num_scalar_prefetch=1,
        grid=(db, n_pages // pps),
        in_specs=[tok, tok, tok,
                  pl.BlockSpec((4, HEAD_DIM), lambda b, s, pt: (0, 0)),
                  pl.BlockSpec((1, HEAD_V), lambda b, s, pt: (0, 0))]
                 + [page_spec(j) for j in range(pps)] * 2,
        out_specs=pl.BlockSpec((None, N_HEADS_A, HEAD_V), lambda b, s, pt: (b, 0, 0)),
        scratch_shapes=[pltpu.VMEM((rows, 1), F32), pltpu.VMEM((rows, 1), F32),
                        pltpu.VMEM((rows, HEAD_V), F32), pltpu.VMEM((rows, HEAD_V), F32)],
    )
    out = pl.pallas_call(
        kernel,
        grid_spec=grid_spec,
        out_shape=jax.ShapeDtypeStruct((db, N_HEADS_A, HEAD_V), F32),
        compiler_params=_params(("parallel", "arbitrary")),
        name="decode_attn",
    )(page_table.reshape(-1), q8, k8, v8, lam_p, og, *([cache_k] * pps), *([cache_v] * pps))
    return out.reshape(db, D_A)


def _conv_tail(conv, lng, lnb, o, x, w_ref, hp=False):
    y = _layer_norm(conv, lng, lnb)
    c = y * jax.nn.sigmoid(y)
    return (x + _mm(o, w_ref[0:D_A, :], hp) + _mm(c, w_ref[D_A:D_A + D_B, :], hp))


def _even_out_kernel(hb_ref, hprev_ref, o_ref, x_ref, wdw_ref, bdw_ref, lng_ref, lnb_ref,
                     w_ref, y_ref, win_ref, conv_ref, *, tm, tiles_per_seq):
    i = pl.program_id(0)
    first = (i % tiles_per_seq) == 0
    win_ref[0:CONV_HALO, :] = jnp.where(first, 0.0, hprev_ref[...])
    win_ref[CONV_HALO:CONV_HALO + tm, :] = hb_ref[...]
    lead = CONV_HALO - (CONV_WIDTH - 1)
    bias = bdw_ref[...]
    for r0 in range(0, tm, CONV_ROWS):
        acc = jnp.broadcast_to(bias, (CONV_ROWS, D_B))
        for j in range(CONV_WIDTH):
            acc = acc + win_ref[r0 + lead + j:r0 + lead + j + CONV_ROWS, :] * wdw_ref[j:j + 1, :]
        conv_ref[r0:r0 + CONV_ROWS, :] = acc
    y_ref[...] = _conv_tail(conv_ref[...], lng_ref[...], lnb_ref[...], o_ref[...], x_ref[...],
                            w_ref)


def _even_out(hb, o, x, wdw, bdw, lng, lnb, w_out, tm, seq):
    T = x.shape[0]
    tiles_per_seq = seq // tm
    halo_blocks = tm // CONV_HALO
    row = lambda i: (i, 0)
    fix = lambda i: (0, 0)
    kernel = functools.partial(_even_out_kernel, tm=tm, tiles_per_seq=tiles_per_seq)
    return pl.pallas_call(
        kernel,
        grid=(T // tm,),
        in_specs=[pl.BlockSpec((tm, D_B), row),
                  pl.BlockSpec((CONV_HALO, D_B), lambda i: (jnp.maximum(i * halo_blocks - 1, 0), 0)),
                  pl.BlockSpec((tm, D_A), row),
                  pl.BlockSpec((tm, D_MODEL), row),
                  pl.BlockSpec((CONV_HALO, D_B), fix),
                  pl.BlockSpec((1, D_B), fix),
                  pl.BlockSpec((1, D_B), fix),
                  pl.BlockSpec((1, D_B), fix),
                  pl.BlockSpec((D_A + D_B, D_MODEL), fix)],
        out_specs=pl.BlockSpec((tm, D_MODEL), row),
        out_shape=jax.ShapeDtypeStruct((T, D_MODEL), F32),
        scratch_shapes=[pltpu.VMEM((CONV_HALO + tm, D_B), F32), pltpu.VMEM((tm, D_B), F32)],
        compiler_params=_params(("parallel",)),
        name="even_out",
    )(hb, hb, o, x, wdw, bdw, lng, lnb, w_out)


def _even_out_sample_kernel(st_ref, hb_ref, o_ref, x_ref, wdw_ref, bdw_ref, lng_ref, lnb_ref,
                            w_ref, y_ref, *, hp):
    w = wdw_ref[...]
    conv = jnp.sum(st_ref[...] * w[0:CONV_WIDTH - 1, :][None], axis=1)
    conv = conv + hb_ref[...] * w[CONV_WIDTH - 1:CONV_WIDTH, :] + bdw_ref[...]
    y_ref[...] = _conv_tail(conv, lng_ref[...], lnb_ref[...], o_ref[...], x_ref[...], w_ref,
                            hp)


def _even_out_sample(state, hb, o, x, wdw, bdw, lng, lnb, w_out, hp):
    db = x.shape[0]
    return pl.pallas_call(
        functools.partial(_even_out_sample_kernel, hp=hp),
        out_shape=jax.ShapeDtypeStruct((db, D_MODEL), F32),
        compiler_params=pltpu.CompilerParams(vmem_limit_bytes=VMEM_LIMIT),
        name="even_out_sample",
    )(state, hb, o, x, wdw, bdw, lng, lnb, w_out)


def _odd_front(x_ref, g_ref, wuv_ref, lng_ref, lnb_ref, hp=False):
    h = _rms(x_ref[...], g_ref[...])
    z = _mm(h, wuv_ref[...], hp)
    z = 0.5 * z * (1.0 + lax.erf(z * (1.0 / math.sqrt(2.0))))
    u = z[:, 0:D_C]
    v = _layer_norm(z[:, D_C:2 * D_C], lng_ref[...], lnb_ref[...])
    return u, v


def _odd_kernel(x_ref, g_ref, wuv_ref, lng_ref, lnb_ref, ws_ref, bs_ref, wo_ref, y_ref,
                sv_ref, *, tm):
    u, v = _odd_front(x_ref, g_ref, wuv_ref, lng_ref, lnb_ref)
    vb = v.astype(BF16)
    r = lax.broadcasted_iota(jnp.int32, (CHUNK, CHUNK), 0)
    c = lax.broadcasted_iota(jnp.int32, (CHUNK, CHUNK), 1)
    gw = D_C // N_GROUPS_C
    for g in range(N_GROUPS_C):
        wsg = jnp.where(c <= r, ws_ref[g], 0.0).astype(BF16)
        for ch in range(tm // CHUNK):
            rows = slice(ch * CHUNK, (ch + 1) * CHUNK)
            cols = slice(g * gw, (g + 1) * gw)
            sv_ref[rows, cols] = _dot(wsg, vb[rows, cols]) + bs_ref[:, cols]
    y_ref[...] = x_ref[...] + _dot((u * sv_ref[...]).astype(BF16), wo_ref[...])


def _odd(x, g, wuv, lng, lnb, ws, bs, wo, tm):
    T = x.shape[0]
    row = lambda i: (i, 0)
    fix = lambda i: (0, 0)
    return pl.pallas_call(
        functools.partial(_odd_kernel, tm=tm),
        grid=(T // tm,),
        in_specs=[pl.BlockSpec((tm, D_MODEL), row),
                  pl.BlockSpec((1, D_MODEL), fix),
                  pl.BlockSpec((D_MODEL, 2 * D_C), fix),
                  pl.BlockSpec((1, D_C), fix),
                  pl.BlockSpec((1, D_C), fix),
                  pl.BlockSpec((N_GROUPS_C, CHUNK, CHUNK), lambda i: (0, 0, 0)),
                  pl.BlockSpec((CHUNK, D_C), fix),
                  pl.BlockSpec((D_C, D_MODEL), fix)],
        out_specs=pl.BlockSpec((tm, D_MODEL), row),
        out_shape=jax.ShapeDtypeStruct((T, D_MODEL), F32),
        scratch_shapes=[pltpu.VMEM((tm, D_C), F32)],
        compiler_params=_params(("parallel",)),
        name="odd_mixer",
    )(x, g, wuv, lng, lnb, ws, bs, wo)


def _odd_sample_kernel(x_ref, g_ref, wuv_ref, lng_ref, lnb_ref, ws0_ref, bs0_ref, wo_ref,
                       y_ref, v_ref, *, hp):
    u, v = _odd_front(x_ref, g_ref, wuv_ref, lng_ref, lnb_ref, hp)
    v_ref[...] = v
    sv = v * ws0_ref[...] + bs0_ref[...]
    y_ref[...] = x_ref[...] + _mm(u * sv, wo_ref[...], hp)


def _odd_sample(x, g, wuv, lng, lnb, ws0, bs0, wo, hp):
    db = x.shape[0]
    return pl.pallas_call(
        functools.partial(_odd_sample_kernel, hp=hp),
        out_shape=[jax.ShapeDtypeStruct((db, D_MODEL), F32),
                   jax.ShapeDtypeStruct((db, D_C), F32)],
        compiler_params=pltpu.CompilerParams(vmem_limit_bytes=VMEM_LIMIT),
        name="odd_mixer_sample",
    )(x, g, wuv, lng, lnb, ws0, bs0, wo)


def _route(logits):
    lane = lax.broadcasted_iota(jnp.int32, logits.shape, 1)
    lane_f = lane.astype(F32)
    none = float(ROUTER_LANES)
    is_g = lane < N_EXPERT_GROUPS
    gl = jnp.where(is_g, logits, NEG_BIG)
    gmax = jnp.max(gl, axis=1, keepdims=True)
    gidx = jnp.min(jnp.where(is_g & (gl == gmax), lane_f, none), axis=1, keepdims=True)
    gw = 1.0 / jnp.sum(jnp.where(is_g, jnp.exp(gl - gmax), 0.0), axis=1, keepdims=True)
    e_f = lane_f - float(N_EXPERT_GROUPS)
    is_e = (lane >= N_EXPERT_GROUPS) & (lane < N_EXPERT_GROUPS + N_EXPERTS)
    grp_lo = gidx * float(EXPERTS_PER_GROUP)
    in_grp = is_e & (e_f >= grp_lo) & (e_f < grp_lo + float(EXPERTS_PER_GROUP))
    cand = jnp.where(in_grp, logits, NEG_BIG)
    v1 = jnp.max(cand, axis=1, keepdims=True)
    i1 = jnp.min(jnp.where(in_grp & (cand == v1), lane_f, none), axis=1, keepdims=True)
    rest = in_grp & (lane_f != i1)
    cand2 = jnp.where(rest, logits, NEG_BIG)
    v2 = jnp.max(cand2, axis=1, keepdims=True)
    i2 = jnp.min(jnp.where(rest & (cand2 == v2), lane_f, none), axis=1, keepdims=True)
    t = jnp.exp(v2 - v1)
    p1 = 1.0 / (1.0 + t)
    p2 = t * p1
    return jnp.where(lane_f == i1, p1 * gw, jnp.where(lane_f == i2, p2 * gw, 0.0))


def _moe_kernel(x_ref, g_ref, wr_ref, br_ref, wup_ref, wdn_ref, y_ref, xn_ref, gate_ref,
                acc_ref, *, hp):
    grp = pl.program_id(1)

    @pl.when(grp == 0)
    def _():
        xn = _rms(x_ref[...], g_ref[...]).astype(xn_ref.dtype)
        xn_ref[...] = xn
        gate_ref[...] = _route(_mm(xn, wr_ref[...], hp) + br_ref[...])
        acc_ref[...] = jnp.zeros(acc_ref.shape, F32)

    xn = xn_ref[...]
    gate = gate_ref[...]
    lane = lax.broadcasted_iota(jnp.int32, gate.shape, 1)
    acc = acc_ref[...]
    for j in range(EXPERTS_PER_GROUP):
        e_lane = N_EXPERT_GROUPS + grp * EXPERTS_PER_GROUP + j
        ge = jnp.sum(jnp.where(lane == e_lane, gate, 0.0), axis=1, keepdims=True)
        hu = _mm(xn, wup_ref[j], hp)
        a = hu[:, 0:D_FF_EXPERT]
        b = hu[:, D_FF_EXPERT:]
        act = a * jax.nn.sigmoid(a) * b * ge
        acc = acc + _mm(act, wdn_ref[j], hp)
    acc_ref[...] = acc

    @pl.when(grp == pl.num_programs(1) - 1)
    def _():
        y_ref[...] = x_ref[...] + acc_ref[...]


def _moe(x, g, wr, br, wup, wdn, tm, hp=False):
    T = x.shape[0]
    row = lambda i, e: (i, 0)
    fix = lambda i, e: (0, 0)
    return pl.pallas_call(
        functools.partial(_moe_kernel, hp=hp),
        grid=(T // tm, N_EXPERT_GROUPS),
        in_specs=[pl.BlockSpec((tm, D_MODEL), row),
                  pl.BlockSpec((1, D_MODEL), fix),
                  pl.BlockSpec((D_MODEL, ROUTER_LANES), fix),
                  pl.BlockSpec((1, ROUTER_LANES), fix),
                  pl.BlockSpec((EXPERTS_PER_GROUP, D_MODEL, 2 * D_FF_EXPERT),
                               lambda i, e: (e, 0, 0)),
                  pl.BlockSpec((EXPERTS_PER_GROUP, D_FF_EXPERT, D_MODEL),
                               lambda i, e: (e, 0, 0))],
        out_specs=pl.BlockSpec((tm, D_MODEL), row),
        out_shape=jax.ShapeDtypeStruct((T, D_MODEL), F32),
        scratch_shapes=[pltpu.VMEM((tm, D_MODEL), F32 if hp else BF16),
                        pltpu.VMEM((tm, ROUTER_LANES), F32),
                        pltpu.VMEM((tm, D_MODEL), F32)],
        compiler_params=_params(("parallel", "arbitrary")),
        name="hier_moe",
    )(x, g, wr, br, wup, wdn)


def _rope_tables(pos):
    half = HEAD_DIM // 2
    inv = ROPE_THETA ** (-jnp.arange(half, dtype=F32) / half)
    ang = pos.astype(F32)[:, None] * inv[None, :]
    cos, sin = jnp.cos(ang), jnp.sin(ang)
    return (jnp.concatenate([cos, cos, cos, cos], axis=1),
            jnp.concatenate([-sin, sin, -sin, sin], axis=1))


def _tile_gain(g, reps):
    return jnp.tile(g.astype(F32), reps)[None, :]


def kernel(x_prompt, x_sample, cache_k, cache_v, state_conv, page_table, norm_mix, norm_ffn, w_in_even, w_out_even, q_norm, k_norm, lambda_qk, head_norm, w_dw, b_dw, conv_ln_g, conv_ln_b, w_uv, v_ln_g, v_ln_b, w_spatial, b_spatial, w_out_odd, w_router_group, b_router_group, w_router_expert, b_router_expert, w_expert_up, w_expert_down):
    batch, seq, _ = x_prompt.shape
    db = x_sample.shape[0]
    n_pages = page_table.shape[1]
    page = cache_k.shape[2]
    past_len = n_pages * page
    tm = TILE_TOKENS

    xp = x_prompt.reshape(batch * seq, D_MODEL)
    xs = x_sample.reshape(db, D_MODEL)
    cos_p, sin_p = _rope_tables(jnp.arange(seq))
    cos_s, sin_s = _rope_tables(jnp.full((db,), past_len))
    ck = cache_k.reshape(cache_k.shape[0], cache_k.shape[1], page * N_HEADS_A, HEAD_V)
    cv = cache_v.reshape(cache_v.shape[0], cache_v.shape[1], page * N_HEADS_A, HEAD_V)
    group_of = jnp.arange(D_A) // HEAD_DIM
    gmat32 = jnp.where(group_of[:, None] == group_of[None, :], 1.0 / HEAD_DIM, 0.0).astype(F32)
    gmat = gmat32.astype(BF16)

    kp_l, vp_l, ks_l, vs_l, cp_l, cs_l, chs_l = [], [], [], [], [], [], []
    for layer in range(DEPTH):
        i = layer // 2
        g_mix = norm_mix[layer][None, :]
        hp = layer < HP_SAMPLE_LAYERS
        if layer % 2 == 0:
            lam_init = 0.8 - 0.6 * math.exp(-0.3 * layer)
            w_in = w_in_even[i].astype(BF16)
            w_out = w_out_even[i].astype(BF16)
            qg = _tile_gain(q_norm[i], D_A // HEAD_DIM)
            kg = _tile_gain(k_norm[i], D_A // HEAD_DIM)
            og = head_norm[i][None, :]
            wdw = jnp.pad(w_dw[i], ((0, CONV_HALO - CONV_WIDTH), (0, 0)))
            bdw, lng, lnb = b_dw[i][None, :], conv_ln_g[i][None, :], conv_ln_b[i][None, :]
            q, k32, kb, v32, vb, hb = _even_in(xp, g_mix, w_in, qg, kg, gmat, cos_p, sin_p,
                                               tm, seq // tm)
            o = _prompt_attention(q, kb, vb, lambda_qk[i], og, batch, seq, lam_init)
            xp = _even_out(hb, o, xp, wdw, bdw, lng, lnb, w_out, tm, seq)
            kp_l.append(k32.reshape(batch, seq, N_HEADS_A, HEAD_V))
            vp_l.append(v32.reshape(batch, seq, N_HEADS_A, HEAD_V))
            cp_l.append(hb.reshape(batch, seq, D_B)[:, seq - (CONV_WIDTH - 1):, :])
            qs, ks32, _, vs32, _, hbs = _even_in(
                xs, g_mix, w_in_even[i] if hp else w_in, qg, kg, gmat32 if hp else gmat,
                cos_s, sin_s, db, 1, hp)
            os_ = _decode_attention(page_table, qs, ks32, vs32, lambda_qk[i], og, ck, cv, i,
                                    lam_init, hp)
            xs = _even_out_sample(state_conv[i], hbs, os_, xs, wdw, bdw, lng,
                                  lnb, w_out_even[i] if hp else w_out, hp)
            ks_l.append(ks32.reshape(db, 1, N_HEADS_A, HEAD_V))
            vs_l.append(vs32.reshape(db, 1, N_HEADS_A, HEAD_V))
            cs_l.append(jnp.concatenate([state_conv[i][:, 1:, :], hbs[:, None, :]], axis=1))
        else:
            wuv = w_uv[i].astype(BF16)
            wo = w_out_odd[i].astype(BF16)
            lng, lnb = v_ln_g[i][None, :], v_ln_b[i][None, :]
            gw = D_C // N_GROUPS_C
            bs = jnp.repeat(b_spatial[i].T, gw, axis=1)
            xp = _odd(xp, g_mix, wuv, lng, lnb, w_spatial[i], bs, wo, tm)
            ws0 = jnp.repeat(w_spatial[i][:, 0, 0], gw)[None, :]
            xs, vch = _odd_sample(xs, g_mix, w_uv[i] if hp else wuv, lng, lnb, ws0, bs[0:1, :],
                                  w_out_odd[i] if hp else wo, hp)
            chs_l.append(vch.reshape(db, 1, D_C))
        g_ffn = norm_ffn[layer][None, :]
        pad = ROUTER_LANES - N_EXPERT_GROUPS - N_EXPERTS
        wr32 = jnp.pad(jnp.concatenate([w_router_group[layer], w_router_expert[layer]], axis=1),
                       ((0, 0), (0, pad)))
        wr = wr32.astype(BF16)
        br = jnp.pad(jnp.concatenate([b_router_group[layer], b_router_expert[layer]]),
                     (0, pad))[None, :]
        wup = w_expert_up[layer].astype(BF16)
        wdn = w_expert_down[layer].astype(BF16)
        xp = _moe(xp, g_ffn, wr, br, wup, wdn, tm)
        if hp:
            xs = _moe(xs, g_ffn, wr32, br, w_expert_up[layer], w_expert_down[layer], db, True)
        else:
            xs = _moe(xs, g_ffn, wr, br, wup, wdn, db)

    return (xp.reshape(batch, seq, D_MODEL), xs.reshape(db, 1, D_MODEL),
            jnp.stack(kp_l), jnp.stack(vp_l), jnp.stack(ks_l), jnp.stack(vs_l),
            jnp.stack(cp_l), jnp.stack(cs_l), jnp.stack(chs_l))
```

```python
import functools
import math

import jax
import jax.numpy as jnp
from jax import lax
from jax.experimental import pallas as pl
from jax.experimental.pallas import tpu as pltpu

F32 = jnp.float32
BF16 = jnp.bfloat16

D_MODEL = 1024
DEPTH = 4
N_HEADS_A = 4
HEAD_DIM = 64
HEAD_V = 2 * HEAD_DIM
D_A = N_HEADS_A * HEAD_V
D_B = D_MODEL // 2
CONV_WIDTH = 31
CONV_HALO = 32
D_IN_EVEN = 3 * D_A + 2 * D_B
D_C = D_MODEL
N_GROUPS_C = 4
CHUNK = 128
N_EXPERT_GROUPS = 4
EXPERTS_PER_GROUP = 4
N_EXPERTS = N_EXPERT_GROUPS * EXPERTS_PER_GROUP
D_FF_EXPERT = 256
ROPE_THETA = 10000.0
EPS = 1e-6
NEG_BIG = -1e30
ROUTER_LANES = 128
SUBLANES = 8

VMEM_LIMIT = 52 * 1024 * 1024

TILE_TOKENS = 512
ATTN_BLOCK = 512
ATTN_SUB = 256
CONV_ROWS = 64
PAGES_PER_STEP = 16
HP_SAMPLE_LAYERS = 2


def _params(sem):
    return pltpu.CompilerParams(dimension_semantics=sem, vmem_limit_bytes=VMEM_LIMIT)


def _rms(x, g):
    return x * lax.rsqrt(jnp.mean(x * x, axis=-1, keepdims=True) + EPS) * g


def _layer_norm(x, g, b):
    mu = jnp.mean(x, axis=-1, keepdims=True)
    xc = x - mu
    var = jnp.mean(xc * xc, axis=-1, keepdims=True)
    return xc * lax.rsqrt(var + EPS) * g + b


def _dot(a, b):
    return jnp.dot(a, b, preferred_element_type=F32)


def _dot_nt(a, b):
    return lax.dot_general(a, b, (((1,), (1,)), ((), ())), preferred_element_type=F32)


def _split(x):
    hi = x.astype(BF16)
    return hi, (x - hi.astype(F32)).astype(BF16)


def _stack_split(x):
    hi = x.astype(BF16).astype(F32)
    return jnp.concatenate([hi, x - hi], axis=0).astype(BF16)


def _mm(a, w, hp):
    if not hp:
        return _dot(a.astype(BF16), w)
    m = a.shape[0]
    wh, wl = _split(w)
    top = _dot(_stack_split(a), wh)
    return top[0:m] + top[m:2 * m] + _dot(a.astype(BF16), wl)


def _lambda(lp, lam_init):
    a = jnp.sum(lp[0:1] * lp[1:2], axis=1, keepdims=True)
    b = jnp.sum(lp[2:3] * lp[3:4], axis=1, keepdims=True)
    return jnp.exp(a) - jnp.exp(b) + lam_init


def _even_in_kernel(x_ref, g_ref, w_ref, qg_ref, kg_ref, gmat_ref, cos_ref, sin_ref,
                    q_ref, k32_ref, kb_ref, v32_ref, vb_ref, hb_ref, *, hp):
    h = _rms(x_ref[...], g_ref[...])
    proj = _mm(h, w_ref[...], hp)
    cos = jnp.concatenate([cos_ref[...]] * (D_A // 128), axis=1)
    sin = jnp.concatenate([sin_ref[...]] * (D_A // 128), axis=1)
    lane = lax.broadcasted_iota(jnp.int32, cos.shape, 1)
    first_half = (lane % HEAD_DIM) < (HEAD_DIM // 2)

    def norm_rope(z, gain):
        ms = _mm(z * z, gmat_ref[...], hp)
        zn = z * lax.rsqrt(ms + EPS) * gain
        rot = jnp.where(first_half,
                        pltpu.roll(zn, D_A - HEAD_DIM // 2, 1),
                        pltpu.roll(zn, HEAD_DIM // 2, 1))
        return zn * cos + rot * sin

    q = norm_rope(proj[:, 0:D_A], qg_ref[...])
    k = norm_rope(proj[:, D_A:2 * D_A], kg_ref[...])
    v = proj[:, 2 * D_A:3 * D_A]
    a = proj[:, 3 * D_A:3 * D_A + D_B]
    g = proj[:, 3 * D_A + D_B:]
    q_ref[...] = (q * (1.0 / math.sqrt(HEAD_DIM))).astype(q_ref.dtype)
    kb_ref[...] = k.astype(BF16)
    vb_ref[...] = v.astype(BF16)
    hb_ref[...] = a * jax.nn.sigmoid(g)
    tm = k.shape[0]
    for h in range(N_HEADS_A):
        cols = slice(h * HEAD_V, (h + 1) * HEAD_V)
        k32_ref[pl.ds(h, tm, stride=N_HEADS_A), :] = k[:, cols]
        v32_ref[pl.ds(h, tm, stride=N_HEADS_A), :] = v[:, cols]


def _even_in(x, g, w_in, slot, qg, kg, gmat, cos, sin, tm, pos_blocks, hp=False):
    T = x.shape[0]
    row = lambda i: (i, 0)
    fix = lambda i: (0, 0)
    pos = lambda i: (i % pos_blocks, 0)
    head_major = jax.ShapeDtypeStruct((T * N_HEADS_A, HEAD_V), F32)
    outs = [jax.ShapeDtypeStruct((T, D_A), F32 if hp else BF16),
            head_major,
            jax.ShapeDtypeStruct((T, D_A), BF16),
            head_major,
            jax.ShapeDtypeStruct((T, D_A), BF16),
            jax.ShapeDtypeStruct((T, D_B), F32)]
    wide = pl.BlockSpec((tm, D_A), row)
    tall = pl.BlockSpec((tm * N_HEADS_A, HEAD_V), row)
    return pl.pallas_call(
        functools.partial(_even_in_kernel, hp=hp),
        grid=(T // tm,),
        in_specs=[pl.BlockSpec((tm, D_MODEL), row),
                  pl.BlockSpec((1, D_MODEL), fix),
                  pl.BlockSpec((None, D_MODEL, D_IN_EVEN), lambda i: (slot, 0, 0)),
                  pl.BlockSpec((1, D_A), fix),
                  pl.BlockSpec((1, D_A), fix),
                  pl.BlockSpec((D_A, D_A), fix),
                  pl.BlockSpec((tm, 128), pos),
                  pl.BlockSpec((tm, 128), pos)],
        out_specs=[wide, tall, wide, tall, wide, pl.BlockSpec((tm, D_B), row)],
        out_shape=outs,
        compiler_params=_params(("parallel",)),
        name="even_in",
    )(x, g, w_in, qg, kg, gmat, cos, sin)


def _attn_kernel(q_ref, k_ref, v_ref, lam_ref, og_ref, o_ref, v1_ref, sa_ref, sb_ref, m_ref,
                 acc_ref, *, blk, sub, lam_init):
    qi = pl.program_id(2)
    per_blk = blk // sub
    assert per_blk % 2 == 0

    @pl.when(qi == 0)
    def _():
        v1_ref[:, 0:HEAD_V] = v_ref[...]
        v1_ref[:, HEAD_V:2 * HEAD_V] = jnp.ones((v1_ref.shape[0], HEAD_V), BF16)

    q = q_ref[...]
    lane = lax.broadcasted_iota(jnp.int32, q.shape, 1)
    zero = jnp.zeros_like(q)
    qq = jnp.concatenate([jnp.where(lane < HEAD_DIM, q, zero),
                          jnp.where(lane >= HEAD_DIM, q, zero)], axis=0)
    m_ref[...] = jnp.full(m_ref.shape, NEG_BIG, F32)
    acc_ref[...] = jnp.zeros(acc_ref.shape, F32)

    def scores(start):
        if not isinstance(start, int):
            start = pl.multiple_of(start, sub)
        return _dot_nt(qq, k_ref[pl.ds(start, sub), :])

    def consume(s, start, mask):
        vblk = v1_ref[pl.ds(start, sub), :]
        if mask is not None:
            s = jnp.where(mask, s, NEG_BIG)
        m_old = m_ref[...]
        m_new = jnp.maximum(m_old, jnp.max(s, axis=1, keepdims=True))
        alpha = jnp.exp(m_old - m_new)
        p = jnp.exp((s - jnp.concatenate([m_new] * (sub // 128), axis=1)).astype(BF16))
        acc_ref[...] = (jnp.concatenate([alpha, alpha], axis=1) * acc_ref[...]
                        + _dot(p, vblk))
        m_ref[...] = m_new

    bufs = (sa_ref, sb_ref)
    sa_ref[...] = scores(0)

    def full_blocks(first, count):
        for u in range(count * per_blk):
            start = pl.multiple_of(first * blk + u * sub, sub)
            bufs[(u + 1) % 2][...] = scores(start + sub)
            consume(bufs[u % 2][...], start, None)

    def body(t, carry):
        full_blocks(2 * t, 2)
        return carry

    lax.fori_loop(0, qi // 2, body, 0)

    @pl.when(qi % 2 == 1)
    def _():
        full_blocks(qi - 1, 1)

    r = lax.broadcasted_iota(jnp.int32, (2 * blk, sub), 0)
    c = lax.broadcasted_iota(jnp.int32, (2 * blk, sub), 1)
    r = jnp.where(r >= blk, r - blk, r)
    for u in range(per_blk):
        start = pl.multiple_of(qi * blk + u * sub, sub)
        if u + 1 < per_blk:
            bufs[(u + 1) % 2][...] = scores(start + sub)
        consume(bufs[u % 2][...], start, c + u * sub <= r)

    lam = _lambda(lam_ref[...], lam_init)
    acc = acc_ref[...]
    ratio = acc[:, 0:HEAD_V] / acc[:, HEAD_V:2 * HEAD_V]
    o = ratio[0:blk] - lam * ratio[blk:2 * blk]
    o_ref[...] = (_rms(o, og_ref[...]) * (1.0 - lam_init)).astype(o_ref.dtype)


def _prompt_attention(q, kb, vb, lam_p, og, batch, seq, lam_init):
    blk = ATTN_BLOCK
    nq = seq // blk
    sub = ATTN_SUB
    kernel = functools.partial(_attn_kernel, blk=blk, sub=sub, lam_init=lam_init)
    scr = [pltpu.VMEM((seq, 2 * HEAD_V), BF16), pltpu.VMEM((2 * blk, sub), F32),
           pltpu.VMEM((2 * blk, sub), F32), pltpu.VMEM((2 * blk, HEAD_V), F32),
           pltpu.VMEM((2 * blk, 2 * HEAD_V), F32)]
    return pl.pallas_call(
        kernel,
        grid=(batch, N_HEADS_A, nq),
        in_specs=[pl.BlockSpec((blk, HEAD_V), lambda b, h, i: (b * nq + i, h)),
                  pl.BlockSpec((seq, HEAD_V), lambda b, h, i: (b, h)),
                  pl.BlockSpec((seq, HEAD_V), lambda b, h, i: (b, h)),
                  pl.BlockSpec((4, HEAD_DIM), lambda b, h, i: (0, 0)),
                  pl.BlockSpec((1, HEAD_V), lambda b, h, i: (0, 0))],
        out_specs=pl.BlockSpec((blk, HEAD_V), lambda b, h, i: (b * nq + i, h)),
        out_shape=jax.ShapeDtypeStruct((batch * seq, D_A), BF16),
        scratch_shapes=scr,
        compiler_params=_params(("parallel", "parallel", "arbitrary")),
        name="prompt_attn",
    )(q, kb, vb, lam_p, og)


def _decode_kernel(pt_ref, q_ref, kn_ref, vn_ref, lam_ref, og_ref, *rest, pps, lam_init, hp):
    k_refs = rest[:pps]
    v_refs = rest[pps:2 * pps]
    o_ref = rest[2 * pps]
    m_ref, l_ref, acc_ref, r_ref = rest[2 * pps + 1:]
    s_id = pl.program_id(1)
    rows = 2 * N_HEADS_A
    prow = k_refs[0].shape[0]

    q8 = q_ref[...]
    qh = q8.astype(BF16)
    if hp:
        q_used = q8
        q_lhs = _stack_split(q8)
    else:
        q_used = qh.astype(F32)
    row = lax.broadcasted_iota(jnp.int32, (rows, prow), 0)
    col = lax.broadcasted_iota(jnp.int32, (rows, prow), 1)
    own = (col % N_HEADS_A) == (row // 2)

    @pl.when(s_id == 0)
    def _():
        m_ref[...] = jnp.full(m_ref.shape, NEG_BIG, F32)
        l_ref[...] = jnp.zeros(l_ref.shape, F32)
        acc_ref[...] = jnp.zeros(acc_ref.shape, F32)

    def scores(k):
        if not hp:
            s = _dot_nt(qh, k.astype(BF16))
        else:
            kh, kl = _split(k)
            top = _dot_nt(q_lhs, kh)
            s = top[0:rows] + top[rows:2 * rows] + _dot_nt(qh, kl)
        return jnp.where(own, s, NEG_BIG)

    s = jnp.concatenate([scores(k_refs[j][...]) for j in range(pps)], axis=1)
    m_old = m_ref[...]
    m_new = jnp.maximum(m_old, jnp.max(s, axis=1, keepdims=True))
    alpha = jnp.exp(m_old - m_new)
    p = jnp.exp(s - m_new)
    l_ref[...] = alpha * l_ref[...] + jnp.sum(p, axis=1, keepdims=True)
    ph = p.astype(BF16)
    if hp:
        p_lhs = _stack_split(p)
    pv = jnp.zeros((rows, HEAD_V), F32)
    for j in range(pps):
        cols = slice(j * prow, (j + 1) * prow)
        v = v_refs[j][...]
        if hp:
            vh, vl = _split(v)
            top = _dot(p_lhs[:, cols], vh)
            pv = pv + top[0:rows] + top[rows:2 * rows] + _dot(ph[:, cols], vl)
        else:
            pv = pv + _dot(ph[:, cols], v.astype(BF16))
    acc_ref[...] = alpha * acc_ref[...] + pv
    m_ref[...] = m_new

    @pl.when(s_id == pl.num_programs(1) - 1)
    def _():
        s_new = jnp.sum(q_used * kn_ref[...], axis=1, keepdims=True)
        m_old = m_ref[...]
        m_fin = jnp.maximum(m_old, s_new)
        alpha = jnp.exp(m_old - m_fin)
        p_new = jnp.exp(s_new - m_fin)
        l_fin = alpha * l_ref[...] + p_new
        r_ref[...] = (alpha * acc_ref[...] + p_new * vn_ref[...]) / l_fin
        lam = _lambda(lam_ref[...], lam_init)
        o = (r_ref[pl.ds(0, N_HEADS_A, stride=2), :]
             - lam * r_ref[pl.ds(1, N_HEADS_A, stride=2), :])
        o_ref[...] = _rms(o, og_ref[...]) * (1.0 - lam_init)


def _decode_attention(page_table, q, k_new, v_new, lam_p, og, cache_k, cache_v, layer_slot,
                      lam_init, hp):
    db, n_pages = page_table.shape
    pps = PAGES_PER_STEP
    prow = cache_k.shape[2]
    rows = 2 * N_HEADS_A
    half = jnp.arange(HEAD_V) // HEAD_DIM
    map_mask = (half[None, :] == jnp.arange(2)[:, None]).astype(F32)
    q8 = (q.astype(F32).reshape(db, N_HEADS_A, 1, HEAD_V) * map_mask).reshape(db, rows, HEAD_V)
    k8 = jnp.repeat(k_new.reshape(db, N_HEADS_A, HEAD_V), 2, axis=1)
    v8 = jnp.repeat(v_new.reshape(db, N_HEADS_A, HEAD_V), 2, axis=1)
    kernel = functools.partial(_decode_kernel, pps=pps, lam_init=lam_init, hp=hp)
    tok = pl.BlockSpec((None, rows, HEAD_V), lambda b, s, pt: (b, 0, 0))

    def page_spec(j):
        return pl.BlockSpec((None, None, prow, HEAD_V),
                            lambda b, s, pt: (layer_slot, pt[b * n_pages + s * pps + j], 0, 0))

    grid_spec = pltpu.PrefetchScalarGridSpec(
        num_scalar_prefetch=1,
        grid=(db, n_pages // pps),
        in_specs=[tok, tok, tok,
                  pl.BlockSpec((4, HEAD_DIM), lambda b, s, pt: (0, 0)),
                  pl.BlockSpec((1, HEAD_V), lambda b, s, pt: (0, 0))]
                 + [page_spec(j) for j in range(pps)] * 2,
        out_specs=pl.BlockSpec((None, N_HEADS_A, HEAD_V), lambda b, s, pt: (b, 0, 0)),
        scratch_shapes=[pltpu.VMEM((rows, 1), F32), pltpu.VMEM((rows, 1), F32),
                        pltpu.VMEM((rows, HEAD_V), F32), pltpu.VMEM((rows, HEAD_V), F32)],
    )
    out = pl.pallas_call(
        kernel,
        grid_spec=grid_spec,
        out_shape=jax.ShapeDtypeStruct((db, N_HEADS_A, HEAD_V), F32),
        compiler_params=_params(("parallel", "arbitrary")),
        name="decode_attn",
    )(page_table.reshape(-1), q8, k8, v8, lam_p, og, *([cache_k] * pps), *([cache_v] * pps))
    return out.reshape(db, D_A)


def _conv_tail(conv, lng, lnb, o, x, w_ref, hp=False):
    y = _layer_norm(conv, lng, lnb)
    c = y * jax.nn.sigmoid(y)
    return (x + _mm(o, w_ref[0:D_A, :], hp) + _mm(c, w_ref[D_A:D_A + D_B, :], hp))


def _even_out_kernel(hb_ref, hprev_ref, o_ref, x_ref, wdw_ref, bdw_ref, lng_ref, lnb_ref,
                     w_ref, y_ref, win_ref, conv_ref, *, tm, tiles_per_seq):
    i = pl.program_id(0)
    first = (i % tiles_per_seq) == 0
    win_ref[0, 0:CONV_HALO, :] = jnp.where(first, 0.0, hprev_ref[...])
    win_ref[0, CONV_HALO:CONV_HALO + tm, :] = hb_ref[...]
    lead = CONV_HALO - (CONV_WIDTH - 1)
    shifted_rows = tm + CONV_HALO - SUBLANES
    for s in range(1, SUBLANES):
        win_ref[s, 0:shifted_rows, :] = win_ref[0, s:s + shifted_rows, :]
    bias = bdw_ref[...]
    for r0 in range(0, tm, CONV_ROWS):
        acc = jnp.broadcast_to(bias, (CONV_ROWS, D_B))
        for j in range(CONV_WIDTH):
            s = (lead + j) % SUBLANES
            a = r0 + lead + j - s
            acc = acc + win_ref[s, a:a + CONV_ROWS, :] * wdw_ref[j:j + 1, :]
        conv_ref[r0:r0 + CONV_ROWS, :] = acc
    y_ref[...] = _conv_tail(conv_ref[...], lng_ref[...], lnb_ref[...], o_ref[...], x_ref[...],
                            w_ref)


def _even_out(hb, o, x, wdw, bdw, lng, lnb, w_out, tm, seq):
    T = x.shape[0]
    tiles_per_seq = seq // tm
    halo_blocks = tm // CONV_HALO
    row = lambda i: (i, 0)
    fix = lambda i: (0, 0)
    kernel = functools.partial(_even_out_kernel, tm=tm, tiles_per_seq=tiles_per_seq)
    return pl.pallas_call(
        kernel,
        grid=(T // tm,),
        in_specs=[pl.BlockSpec((tm, D_B), row),
                  pl.BlockSpec((CONV_HALO, D_B), lambda i: (jnp.maximum(i * halo_blocks - 1, 0), 0)),
                  pl.BlockSpec((tm, D_A), row),
                  pl.BlockSpec((tm, D_MODEL), row),
                  pl.BlockSpec((CONV_HALO, D_B), fix),
                  pl.BlockSpec((1, D_B), fix),
                  pl.BlockSpec((1, D_B), fix),
                  pl.BlockSpec((1, D_B), fix),
                  pl.BlockSpec((D_A + D_B, D_MODEL), fix)],
        out_specs=pl.BlockSpec((tm, D_MODEL), row),
        out_shape=jax.ShapeDtypeStruct((T, D_MODEL), F32),
        scratch_shapes=[pltpu.VMEM((SUBLANES, CONV_HALO + tm, D_B), F32),
                        pltpu.VMEM((tm, D_B), F32)],
        compiler_params=_params(("parallel",)),
        name="even_out",
    )(hb, hb, o, x, wdw, bdw, lng, lnb, w_out)


def _even_out_sample_kernel(st_ref, hb_ref, o_ref, x_ref, wdw_ref, bdw_ref, lng_ref, lnb_ref,
                            w_ref, y_ref, *, hp):
    w = wdw_ref[...]
    conv = jnp.sum(st_ref[...] * w[0:CONV_WIDTH - 1, :][None], axis=1)
    conv = conv + hb_ref[...] * w[CONV_WIDTH - 1:CONV_WIDTH, :] + bdw_ref[...]
    y_ref[...] = _conv_tail(conv, lng_ref[...], lnb_ref[...], o_ref[...], x_ref[...], w_ref,
                            hp)


def _even_out_sample(state, hb, o, x, wdw, bdw, lng, lnb, w_out, hp):
    db = x.shape[0]
    return pl.pallas_call(
        functools.partial(_even_out_sample_kernel, hp=hp),
        out_shape=jax.ShapeDtypeStruct((db, D_MODEL), F32),
        compiler_params=pltpu.CompilerParams(vmem_limit_bytes=VMEM_LIMIT),
        name="even_out_sample",
    )(state, hb, o, x, wdw, bdw, lng, lnb, w_out)


def _odd_front(x_ref, g_ref, wuv_ref, lng_ref, lnb_ref, hp=False):
    h = _rms(x_ref[...], g_ref[...])
    z = _mm(h, wuv_ref[...], hp)
    z = 0.5 * z * (1.0 + lax.erf(z * (1.0 / math.sqrt(2.0))))
    u = z[:, 0:D_C]
    v = _layer_norm(z[:, D_C:2 * D_C], lng_ref[...], lnb_ref[...])
    return u, v


def _odd_kernel(x_ref, g_ref, wuv_ref, lng_ref, lnb_ref, ws_ref, bs_ref, wo_ref, y_ref,
                sv_ref, *, tm):
    u, v = _odd_front(x_ref, g_ref, wuv_ref, lng_ref, lnb_ref)
    vb = v.astype(BF16)
    r = lax.broadcasted_iota(jnp.int32, (CHUNK, CHUNK), 0)
    c = lax.broadcasted_iota(jnp.int32, (CHUNK, CHUNK), 1)
    gw = D_C // N_GROUPS_C
    for g in range(N_GROUPS_C):
        wsg = jnp.where(c <= r, ws_ref[g], 0.0).astype(BF16)
        for ch in range(tm // CHUNK):
            rows = slice(ch * CHUNK, (ch + 1) * CHUNK)
            cols = slice(g * gw, (g + 1) * gw)
            sv_ref[rows, cols] = _dot(wsg, vb[rows, cols]) + bs_ref[:, cols]
    y_ref[...] = x_ref[...] + _dot((u * sv_ref[...]).astype(BF16), wo_ref[...])


def _odd(x, g, wuv, lng, lnb, ws, bs, wo, tm):
    T = x.shape[0]
    row = lambda i: (i, 0)
    fix = lambda i: (0, 0)
    return pl.pallas_call(
        functools.partial(_odd_kernel, tm=tm),
        grid=(T // tm,),
        in_specs=[pl.BlockSpec((tm, D_MODEL), row),
                  pl.BlockSpec((1, D_MODEL), fix),
                  pl.BlockSpec((D_MODEL, 2 * D_C), fix),
                  pl.BlockSpec((1, D_C), fix),
                  pl.BlockSpec((1, D_C), fix),
                  pl.BlockSpec((N_GROUPS_C, CHUNK, CHUNK), lambda i: (0, 0, 0)),
                  pl.BlockSpec((CHUNK, D_C), fix),
                  pl.BlockSpec((D_C, D_MODEL), fix)],
        out_specs=pl.BlockSpec((tm, D_MODEL), row),
        out_shape=jax.ShapeDtypeStruct((T, D_MODEL), F32),
        scratch_shapes=[pltpu.VMEM((tm, D_C), F32)],
        compiler_params=_params(("parallel",)),
        name="odd_mixer",
    )(x, g, wuv, lng, lnb, ws, bs, wo)


def _odd_sample_kernel(x_ref, g_ref, wuv_ref, lng_ref, lnb_ref, ws0_ref, bs0_ref, wo_ref,
                       y_ref, v_ref, *, hp):
    u, v = _odd_front(x_ref, g_ref, wuv_ref, lng_ref, lnb_ref, hp)
    v_ref[...] = v
    sv = v * ws0_ref[...] + bs0_ref[...]
    y_ref[...] = x_ref[...] + _mm(u * sv, wo_ref[...], hp)


def _odd_sample(x, g, wuv, lng, lnb, ws0, bs0, wo, hp):
    db = x.shape[0]
    return pl.pallas_call(
        functools.partial(_odd_sample_kernel, hp=hp),
        out_shape=[jax.ShapeDtypeStruct((db, D_MODEL), F32),
                   jax.ShapeDtypeStruct((db, D_C), F32)],
        compiler_params=pltpu.CompilerParams(vmem_limit_bytes=VMEM_LIMIT),
        name="odd_mixer_sample",
    )(x, g, wuv, lng, lnb, ws0, bs0, wo)


def _route(logits):
    lane = lax.broadcasted_iota(jnp.int32, logits.shape, 1)
    lane_f = lane.astype(F32)
    none = float(ROUTER_LANES)
    is_g = lane < N_EXPERT_GROUPS
    gl = jnp.where(is_g, logits, NEG_BIG)
    gmax = jnp.max(gl, axis=1, keepdims=True)
    gidx = jnp.min(jnp.where(is_g & (gl == gmax), lane_f, none), axis=1, keepdims=True)
    gw = 1.0 / jnp.sum(jnp.where(is_g, jnp.exp(gl - gmax), 0.0), axis=1, keepdims=True)
    e_f = lane_f - float(N_EXPERT_GROUPS)
    is_e = (lane >= N_EXPERT_GROUPS) & (lane < N_EXPERT_GROUPS + N_EXPERTS)
    grp_lo = gidx * float(EXPERTS_PER_GROUP)
    in_grp = is_e & (e_f >= grp_lo) & (e_f < grp_lo + float(EXPERTS_PER_GROUP))
    cand = jnp.where(in_grp, logits, NEG_BIG)
    v1 = jnp.max(cand, axis=1, keepdims=True)
    i1 = jnp.min(jnp.where(in_grp & (cand == v1), lane_f, none), axis=1, keepdims=True)
    rest = in_grp & (lane_f != i1)
    cand2 = jnp.where(rest, logits, NEG_BIG)
    v2 = jnp.max(cand2, axis=1, keepdims=True)
    i2 = jnp.min(jnp.where(rest & (cand2 == v2), lane_f, none), axis=1, keepdims=True)
    t = jnp.exp(v2 - v1)
    p1 = 1.0 / (1.0 + t)
    p2 = t * p1
    return jnp.where(lane_f == i1, p1 * gw, jnp.where(lane_f == i2, p2 * gw, 0.0))


def _moe_kernel(x_ref, g_ref, wr_ref, br_ref, wup_ref, wdn_ref, y_ref, xn_ref, gate_ref,
                acc_ref, *, hp):
    grp = pl.program_id(1)

    @pl.when(grp == 0)
    def _():
        xn = _rms(x_ref[...], g_ref[...]).astype(xn_ref.dtype)
        xn_ref[...] = xn
        gate_ref[...] = _route(_mm(xn, wr_ref[...], hp) + br_ref[...])
        acc_ref[...] = jnp.zeros(acc_ref.shape, F32)

    xn = xn_ref[...]
    gate = gate_ref[...]
    lane = lax.broadcasted_iota(jnp.int32, gate.shape, 1)
    acc = acc_ref[...]
    for j in range(EXPERTS_PER_GROUP):
        e_lane = N_EXPERT_GROUPS + grp * EXPERTS_PER_GROUP + j
        ge = jnp.sum(jnp.where(lane == e_lane, gate, 0.0), axis=1, keepdims=True)
        hu = _mm(xn, wup_ref[j], hp)
        a = hu[:, 0:D_FF_EXPERT]
        b = hu[:, D_FF_EXPERT:]
        act = a * jax.nn.sigmoid(a) * b * ge
        acc = acc + _mm(act, wdn_ref[j], hp)
    acc_ref[...] = acc

    @pl.when(grp == pl.num_programs(1) - 1)
    def _():
        y_ref[...] = x_ref[...] + acc_ref[...]


def _moe(x, g, wr, br, wup, wdn, layer, tm, hp=False):
    T = x.shape[0]
    row = lambda i, e: (i, 0)
    fix = lambda i, e: (0, 0)
    return pl.pallas_call(
        functools.partial(_moe_kernel, hp=hp),
        grid=(T // tm, N_EXPERT_GROUPS),
        in_specs=[pl.BlockSpec((tm, D_MODEL), row),
                  pl.BlockSpec((1, D_MODEL), fix),
                  pl.BlockSpec((D_MODEL, ROUTER_LANES), fix),
                  pl.BlockSpec((1, ROUTER_LANES), fix),
                  pl.BlockSpec((None, EXPERTS_PER_GROUP, D_MODEL, 2 * D_FF_EXPERT),
                               lambda i, e: (layer, e, 0, 0)),
                  pl.BlockSpec((None, EXPERTS_PER_GROUP, D_FF_EXPERT, D_MODEL),
                               lambda i, e: (layer, e, 0, 0))],
        out_specs=pl.BlockSpec((tm, D_MODEL), row),
        out_shape=jax.ShapeDtypeStruct((T, D_MODEL), F32),
        scratch_shapes=[pltpu.VMEM((tm, D_MODEL), F32 if hp else BF16),
                        pltpu.VMEM((tm, ROUTER_LANES), F32),
                        pltpu.VMEM((tm, D_MODEL), F32)],
        compiler_params=_params(("parallel", "arbitrary")),
        name="hier_moe",
    )(x, g, wr, br, wup, wdn)


def _rope_tables(pos):
    half = HEAD_DIM // 2
    inv = ROPE_THETA ** (-jnp.arange(half, dtype=F32) / half)
    ang = pos.astype(F32)[:, None] * inv[None, :]
    cos, sin = jnp.cos(ang), jnp.sin(ang)
    return (jnp.concatenate([cos, cos, cos, cos], axis=1),
            jnp.concatenate([-sin, sin, -sin, sin], axis=1))


def _tile_gain(g, reps):
    return jnp.tile(g.astype(F32), reps)[None, :]


def kernel(x_prompt, x_sample, cache_k, cache_v, state_conv, page_table, norm_mix, norm_ffn, w_in_even, w_out_even, q_norm, k_norm, lambda_qk, head_norm, w_dw, b_dw, conv_ln_g, conv_ln_b, w_uv, v_ln_g, v_ln_b, w_spatial, b_spatial, w_out_odd, w_router_group, b_router_group, w_router_expert, b_router_expert, w_expert_up, w_expert_down):
    batch, seq, _ = x_prompt.shape
    db = x_sample.shape[0]
    n_pages = page_table.shape[1]
    page = cache_k.shape[2]
    past_len = n_pages * page
    tm = TILE_TOKENS

    xp = x_prompt.reshape(batch * seq, D_MODEL)
    xs = x_sample.reshape(db, D_MODEL)
    cos_p, sin_p = _rope_tables(jnp.arange(seq))
    cos_s, sin_s = _rope_tables(jnp.full((db,), past_len))
    ck = cache_k.reshape(cache_k.shape[0], cache_k.shape[1], page * N_HEADS_A, HEAD_V)
    cv = cache_v.reshape(cache_v.shape[0], cache_v.shape[1], page * N_HEADS_A, HEAD_V)
    group_of = jnp.arange(D_A) // HEAD_DIM
    gmat32 = jnp.where(group_of[:, None] == group_of[None, :], 1.0 / HEAD_DIM, 0.0).astype(F32)
    gmat = gmat32.astype(BF16)
    w_in_bf = w_in_even.astype(BF16)
    wup_bf = w_expert_up.astype(BF16)
    wdn_bf = w_expert_down.astype(BF16)

    kp_l, vp_l, ks_l, vs_l, cp_l, cs_l, chs_l = [], [], [], [], [], [], []
    for layer in range(DEPTH):
        i = layer // 2
        g_mix = norm_mix[layer][None, :]
        hp = layer < HP_SAMPLE_LAYERS
        if layer % 2 == 0:
            lam_init = 0.8 - 0.6 * math.exp(-0.3 * layer)
            w_out = w_out_even[i].astype(BF16)
            qg = _tile_gain(q_norm[i], D_A // HEAD_DIM)
            kg = _tile_gain(k_norm[i], D_A // HEAD_DIM)
            og = head_norm[i][None, :]
            wdw = jnp.pad(w_dw[i], ((0, CONV_HALO - CONV_WIDTH), (0, 0)))
            bdw, lng, lnb = b_dw[i][None, :], conv_ln_g[i][None, :], conv_ln_b[i][None, :]
            q, k32, kb, v32, vb, hb = _even_in(xp, g_mix, w_in_bf, i, qg, kg, gmat, cos_p, sin_p,
                                               tm, seq // tm)
            o = _prompt_attention(q, kb, vb, lambda_qk[i], og, batch, seq, lam_init)
            xp = _even_out(hb, o, xp, wdw, bdw, lng, lnb, w_out, tm, seq)
            kp_l.append(k32.reshape(batch, seq, N_HEADS_A, HEAD_V))
            vp_l.append(v32.reshape(batch, seq, N_HEADS_A, HEAD_V))
            cp_l.append(hb.reshape(batch, seq, D_B)[:, seq - (CONV_WIDTH - 1):, :])
            qs, ks32, _, vs32, _, hbs = _even_in(
                xs, g_mix, w_in_even if hp else w_in_bf, i, qg, kg, gmat32 if hp else gmat,
                cos_s, sin_s, db, 1, hp)
            os_ = _decode_attention(page_table, qs, ks32, vs32, lambda_qk[i], og, ck, cv, i,
                                    lam_init, hp)
            xs = _even_out_sample(state_conv[i], hbs, os_, xs, wdw, bdw, lng,
                                  lnb, w_out_even[i] if hp else w_out, hp)
            ks_l.append(ks32.reshape(db, 1, N_HEADS_A, HEAD_V))
            vs_l.append(vs32.reshape(db, 1, N_HEADS_A, HEAD_V))
            cs_l.append(jnp.concatenate([state_conv[i][:, 1:, :], hbs[:, None, :]], axis=1))
        else:
            wuv = w_uv[i].astype(BF16)
            wo = w_out_odd[i].astype(BF16)
            lng, lnb = v_ln_g[i][None, :], v_ln_b[i][None, :]
            gw = D_C // N_GROUPS_C
            bs = jnp.repeat(b_spatial[i].T, gw, axis=1)
            xp = _odd(xp, g_mix, wuv, lng, lnb, w_spatial[i], bs, wo, tm)
            ws0 = jnp.repeat(w_spatial[i][:, 0, 0], gw)[None, :]
            xs, vch = _odd_sample(xs, g_mix, w_uv[i] if hp else wuv, lng, lnb, ws0, bs[0:1, :],
                                  w_out_odd[i] if hp else wo, hp)
            chs_l.append(vch.reshape(db, 1, D_C))
        g_ffn = norm_ffn[layer][None, :]
        pad = ROUTER_LANES - N_EXPERT_GROUPS - N_EXPERTS
        wr32 = jnp.pad(jnp.concatenate([w_router_group[layer], w_router_expert[layer]], axis=1),
                       ((0, 0), (0, pad)))
        wr = wr32.astype(BF16)
        br = jnp.pad(jnp.concatenate([b_router_group[layer], b_router_expert[layer]]),
                     (0, pad))[None, :]
        xp = _moe(xp, g_ffn, wr, br, wup_bf, wdn_bf, layer, tm)
        if hp:
            xs = _moe(xs, g_ffn, wr32, br, w_expert_up, w_expert_down, layer, db, True)
        else:
            xs = _moe(xs, g_ffn, wr, br, wup_bf, wdn_bf, layer, db)

    return (xp.reshape(batch, seq, D_MODEL), xs.reshape(db, 1, D_MODEL),
            jnp.stack(kp_l), jnp.stack(vp_l), jnp.stack(ks_l), jnp.stack(vs_l),
            jnp.stack(cp_l), jnp.stack(cs_l), jnp.stack(chs_l))
```

```python
import functools
import math

import jax
import jax.numpy as jnp
from jax import lax
from jax.experimental import pallas as pl
from jax.experimental.pallas import tpu as pltpu

F32 = jnp.float32
BF16 = jnp.bfloat16

D_MODEL = 1024
DEPTH = 4
N_HEADS_A = 4
HEAD_DIM = 64
HEAD_V = 2 * HEAD_DIM
D_A = N_HEADS_A * HEAD_V
D_B = D_MODEL // 2
CONV_WIDTH = 31
CONV_HALO = 32
D_IN_EVEN = 3 * D_A + 2 * D_B
D_C = D_MODEL
N_GROUPS_C = 4
CHUNK = 128
N_EXPERT_GROUPS = 4
EXPERTS_PER_GROUP = 4
N_EXPERTS = N_EXPERT_GROUPS * EXPERTS_PER_GROUP
D_FF_EXPERT = 256
ROPE_THETA = 10000.0
EPS = 1e-6
NEG_BIG = -1e30
ROUTER_LANES = 128
SUBLANES = 8

VMEM_LIMIT = 52 * 1024 * 1024

TILE_TOKENS = 512
MOE_TILE_TOKENS = 1024
ATTN_BLOCK = 512
ATTN_SUB = 256
CONV_ROWS = 64
PAGES_PER_STEP = 32
HP_SAMPLE_LAYERS = 2


def _params(sem):
    return pltpu.CompilerParams(dimension_semantics=sem, vmem_limit_bytes=VMEM_LIMIT)


def _rms(x, g):
    return x * lax.rsqrt(jnp.mean(x * x, axis=-1, keepdims=True) + EPS) * g


def _layer_norm(x, g, b):
    mu = jnp.mean(x, axis=-1, keepdims=True)
    xc = x - mu
    var = jnp.mean(xc * xc, axis=-1, keepdims=True)
    return xc * lax.rsqrt(var + EPS) * g + b


def _dot(a, b):
    return jnp.dot(a, b, preferred_element_type=F32)


def _dot_nt(a, b):
    return lax.dot_general(a, b, (((1,), (1,)), ((), ())), preferred_element_type=F32)


def _split(x):
    hi = x.astype(BF16)
    return hi, (x - hi.astype(F32)).astype(BF16)


def _stack_split(x):
    hi = x.astype(BF16).astype(F32)
    return jnp.concatenate([hi, x - hi], axis=0).astype(BF16)


def _mm(a, w, hp):
    if not hp:
        return _dot(a.astype(BF16), w)
    m = a.shape[0]
    wh, wl = _split(w)
    top = _dot(_stack_split(a), wh)
    return top[0:m] + top[m:2 * m] + _dot(a.astype(BF16), wl)


def _lambda(lp, lam_init):
    a = jnp.sum(lp[0:1] * lp[1:2], axis=1, keepdims=True)
    b = jnp.sum(lp[2:3] * lp[3:4], axis=1, keepdims=True)
    return jnp.exp(a) - jnp.exp(b) + lam_init


def _even_in_kernel(x_ref, g_ref, w_ref, qg_ref, kg_ref, gmat_ref, cos_ref, sin_ref,
                    q_ref, k32_ref, kb_ref, v32_ref, vb_ref, hb_ref, *, hp):
    h = _rms(x_ref[...], g_ref[...])
    proj = _mm(h, w_ref[...], hp)
    cos = jnp.concatenate([cos_ref[...]] * (D_A // 128), axis=1)
    sin = jnp.concatenate([sin_ref[...]] * (D_A // 128), axis=1)
    lane = lax.broadcasted_iota(jnp.int32, cos.shape, 1)
    first_half = (lane % HEAD_DIM) < (HEAD_DIM // 2)

    def norm_rope(z, gain):
        ms = _mm(z * z, gmat_ref[...], hp)
        zn = z * lax.rsqrt(ms + EPS) * gain
        rot = jnp.where(first_half,
                        pltpu.roll(zn, D_A - HEAD_DIM // 2, 1),
                        pltpu.roll(zn, HEAD_DIM // 2, 1))
        return zn * cos + rot * sin

    q = norm_rope(proj[:, 0:D_A], qg_ref[...])
    k = norm_rope(proj[:, D_A:2 * D_A], kg_ref[...])
    v = proj[:, 2 * D_A:3 * D_A]
    a = proj[:, 3 * D_A:3 * D_A + D_B]
    g = proj[:, 3 * D_A + D_B:]
    q_ref[...] = (q * (1.0 / math.sqrt(HEAD_DIM))).astype(q_ref.dtype)
    kb_ref[...] = k.astype(BF16)
    vb_ref[...] = v.astype(BF16)
    hb_ref[...] = a * jax.nn.sigmoid(g)
    tm = k.shape[0]
    for h in range(N_HEADS_A):
        cols = slice(h * HEAD_V, (h + 1) * HEAD_V)
        k32_ref[pl.ds(h, tm, stride=N_HEADS_A), :] = k[:, cols]
        v32_ref[pl.ds(h, tm, stride=N_HEADS_A), :] = v[:, cols]


def _even_in(x, g, w_in, slot, qg, kg, gmat, cos, sin, tm, pos_blocks, hp=False):
    T = x.shape[0]
    row = lambda i: (i, 0)
    fix = lambda i: (0, 0)
    pos = lambda i: (i % pos_blocks, 0)
    head_major = jax.ShapeDtypeStruct((T * N_HEADS_A, HEAD_V), F32)
    outs = [jax.ShapeDtypeStruct((T, D_A), F32 if hp else BF16),
            head_major,
            jax.ShapeDtypeStruct((T, D_A), BF16),
            head_major,
            jax.ShapeDtypeStruct((T, D_A), BF16),
            jax.ShapeDtypeStruct((T, D_B), F32)]
    wide = pl.BlockSpec((tm, D_A), row)
    tall = pl.BlockSpec((tm * N_HEADS_A, HEAD_V), row)
    return pl.pallas_call(
        functools.partial(_even_in_kernel, hp=hp),
        grid=(T // tm,),
        in_specs=[pl.BlockSpec((tm, D_MODEL), row),
                  pl.BlockSpec((1, D_MODEL), fix),
                  pl.BlockSpec((None, D_MODEL, D_IN_EVEN), lambda i: (slot, 0, 0)),
                  pl.BlockSpec((1, D_A), fix),
                  pl.BlockSpec((1, D_A), fix),
                  pl.BlockSpec((D_A, D_A), fix),
                  pl.BlockSpec((tm, 128), pos),
                  pl.BlockSpec((tm, 128), pos)],
        out_specs=[wide, tall, wide, tall, wide, pl.BlockSpec((tm, D_B), row)],
        out_shape=outs,
        compiler_params=_params(("parallel",)),
        name="even_in",
    )(x, g, w_in, qg, kg, gmat, cos, sin)


def _attn_kernel(q_ref, k_ref, v_ref, lam_ref, og_ref, o_ref, v1_ref, sa_ref, sb_ref, m_ref,
                 acc_ref, *, blk, sub, lam_init):
    qi = pl.program_id(2)
    per_blk = blk // sub
    assert per_blk % 2 == 0

    @pl.when(qi == 0)
    def _():
        v1_ref[:, 0:HEAD_V] = v_ref[...]
        v1_ref[:, HEAD_V:2 * HEAD_V] = jnp.ones((v1_ref.shape[0], HEAD_V), BF16)

    q = q_ref[...]
    lane = lax.broadcasted_iota(jnp.int32, q.shape, 1)
    zero = jnp.zeros_like(q)
    qq = jnp.concatenate([jnp.where(lane < HEAD_DIM, q, zero),
                          jnp.where(lane >= HEAD_DIM, q, zero)], axis=0)
    m_ref[...] = jnp.full(m_ref.shape, NEG_BIG, F32)
    acc_ref[...] = jnp.zeros(acc_ref.shape, F32)

    def scores(start):
        if not isinstance(start, int):
            start = pl.multiple_of(start, sub)
        return _dot_nt(qq, k_ref[pl.ds(start, sub), :])

    def consume(s, start, mask):
        vblk = v1_ref[pl.ds(start, sub), :]
        if mask is not None:
            s = jnp.where(mask, s, NEG_BIG)
        m_old = m_ref[...]
        m_new = jnp.maximum(m_old, jnp.max(s, axis=1, keepdims=True))
        alpha = jnp.exp(m_old - m_new)
        p = jnp.exp((s - jnp.concatenate([m_new] * (sub // 128), axis=1)).astype(BF16))
        acc_ref[...] = (jnp.concatenate([alpha, alpha], axis=1) * acc_ref[...]
                        + _dot(p, vblk))
        m_ref[...] = m_new

    bufs = (sa_ref, sb_ref)
    sa_ref[...] = scores(0)

    def full_blocks(first, count):
        for u in range(count * per_blk):
            start = pl.multiple_of(first * blk + u * sub, sub)
            bufs[(u + 1) % 2][...] = scores(start + sub)
            consume(bufs[u % 2][...], start, None)

    def body(t, carry):
        full_blocks(2 * t, 2)
        return carry

    lax.fori_loop(0, qi // 2, body, 0)

    @pl.when(qi % 2 == 1)
    def _():
        full_blocks(qi - 1, 1)

    r = lax.broadcasted_iota(jnp.int32, (2 * blk, sub), 0)
    c = lax.broadcasted_iota(jnp.int32, (2 * blk, sub), 1)
    r = jnp.where(r >= blk, r - blk, r)
    for u in range(per_blk):
        start = pl.multiple_of(qi * blk + u * sub, sub)
        if u + 1 < per_blk:
            bufs[(u + 1) % 2][...] = scores(start + sub)
        consume(bufs[u % 2][...], start, c + u * sub <= r)

    lam = _lambda(lam_ref[...], lam_init)
    acc = acc_ref[...]
    ratio = acc[:, 0:HEAD_V] / acc[:, HEAD_V:2 * HEAD_V]
    o = ratio[0:blk] - lam * ratio[blk:2 * blk]
    o_ref[...] = (_rms(o, og_ref[...]) * (1.0 - lam_init)).astype(o_ref.dtype)


def _prompt_attention(q, kb, vb, lam_p, og, batch, seq, lam_init):
    blk = ATTN_BLOCK
    nq = seq // blk
    sub = ATTN_SUB
    kernel = functools.partial(_attn_kernel, blk=blk, sub=sub, lam_init=lam_init)
    scr = [pltpu.VMEM((seq, 2 * HEAD_V), BF16), pltpu.VMEM((2 * blk, sub), F32),
           pltpu.VMEM((2 * blk, sub), F32), pltpu.VMEM((2 * blk, HEAD_V), F32),
           pltpu.VMEM((2 * blk, 2 * HEAD_V), F32)]
    return pl.pallas_call(
        kernel,
        grid=(batch, N_HEADS_A, nq),
        in_specs=[pl.BlockSpec((blk, HEAD_V), lambda b, h, i: (b * nq + i, h)),
                  pl.BlockSpec((seq, HEAD_V), lambda b, h, i: (b, h)),
                  pl.BlockSpec((seq, HEAD_V), lambda b, h, i: (b, h)),
                  pl.BlockSpec((4, HEAD_DIM), lambda b, h, i: (0, 0)),
                  pl.BlockSpec((1, HEAD_V), lambda b, h, i: (0, 0))],
        out_specs=pl.BlockSpec((blk, HEAD_V), lambda b, h, i: (b * nq + i, h)),
        out_shape=jax.ShapeDtypeStruct((batch * seq, D_A), BF16),
        scratch_shapes=scr,
        compiler_params=_params(("parallel", "parallel", "arbitrary")),
        name="prompt_attn",
    )(q, kb, vb, lam_p, og)


def _decode_kernel(pt_ref, q_ref, kn_ref, vn_ref, lam_ref, og_ref, *rest, pps, lam_init, hp):
    k_refs = rest[:pps]
    v_refs = rest[pps:2 * pps]
    o_ref = rest[2 * pps]
    m_ref, l_ref, acc_ref, r_ref = rest[2 * pps + 1:]
    s_id = pl.program_id(1)
    rows = 2 * N_HEADS_A
    prow = k_refs[0].shape[0]

    q8 = q_ref[...]
    qh = q8.astype(BF16)
    if hp:
        q_used = q8
        q_lhs = _stack_split(q8)
    else:
        q_used = qh.astype(F32)
    row = lax.broadcasted_iota(jnp.int32, (rows, prow), 0)
    col = lax.broadcasted_iota(jnp.int32, (rows, prow), 1)
    own = (col % N_HEADS_A) == (row // 2)

    @pl.when(s_id == 0)
    def _():
        m_ref[...] = jnp.full(m_ref.shape, NEG_BIG, F32)
        l_ref[...] = jnp.zeros(l_ref.shape, F32)
        acc_ref[...] = jnp.zeros(acc_ref.shape, F32)

    def scores(k):
        if not hp:
            s = _dot_nt(qh, k.astype(BF16))
        else:
            kh, kl = _split(k)
            top = _dot_nt(q_lhs, kh)
            s = top[0:rows] + top[rows:2 * rows] + _dot_nt(qh, kl)
        return jnp.where(own, s, NEG_BIG)

    s = jnp.concatenate([scores(k_refs[j][...]) for j in range(pps)], axis=1)
    m_old = m_ref[...]
    m_new = jnp.maximum(m_old, jnp.max(s, axis=1, keepdims=True))
    alpha = jnp.exp(m_old - m_new)
    p = jnp.exp(s - m_new)
    l_ref[...] = alpha * l_ref[...] + jnp.sum(p, axis=1, keepdims=True)
    ph = p.astype(BF16)
    if hp:
        p_lhs = _stack_split(p)
    pv = jnp.zeros((rows, HEAD_V), F32)
    for j in range(pps):
        cols = slice(j * prow, (j + 1) * prow)
        v = v_refs[j][...]
        if hp:
            vh, vl = _split(v)
            top = _dot(p_lhs[:, cols], vh)
            pv = pv + top[0:rows] + top[rows:2 * rows] + _dot(ph[:, cols], vl)
        else:
            pv = pv + _dot(ph[:, cols], v.astype(BF16))
    acc_ref[...] = alpha * acc_ref[...] + pv
    m_ref[...] = m_new

    @pl.when(s_id == pl.num_programs(1) - 1)
    def _():
        s_new = jnp.sum(q_used * kn_ref[...], axis=1, keepdims=True)
        m_old = m_ref[...]
        m_fin = jnp.maximum(m_old, s_new)
        alpha = jnp.exp(m_old - m_fin)
        p_new = jnp.exp(s_new - m_fin)
        l_fin = alpha * l_ref[...] + p_new
        r_ref[...] = (alpha * acc_ref[...] + p_new * vn_ref[...]) / l_fin
        lam = _lambda(lam_ref[...], lam_init)
        o = (r_ref[pl.ds(0, N_HEADS_A, stride=2), :]
             - lam * r_ref[pl.ds(1, N_HEADS_A, stride=2), :])
        o_ref[...] = _rms(o, og_ref[...]) * (1.0 - lam_init)


def _decode_attention(page_table, q, k_new, v_new, lam_p, og, cache_k, cache_v, layer_slot,
                      lam_init, hp):
    db, n_pages = page_table.shape
    pps = PAGES_PER_STEP
    prow = cache_k.shape[2]
    rows = 2 * N_HEADS_A
    half = jnp.arange(HEAD_V) // HEAD_DIM
    map_mask = (half[None, :] == jnp.arange(2)[:, None]).astype(F32)
    q8 = (q.astype(F32).reshape(db, N_HEADS_A, 1, HEAD_V) * map_mask).reshape(db, rows, HEAD_V)
    k8 = jnp.repeat(k_new.reshape(db, N_HEADS_A, HEAD_V), 2, axis=1)
    v8 = jnp.repeat(v_new.reshape(db, N_HEADS_A, HEAD_V), 2, axis=1)
    kernel = functools.partial(_decode_kernel, pps=pps, lam_init=lam_init, hp=hp)
    tok = pl.BlockSpec((None, rows, HEAD_V), lambda b, s, pt: (b, 0, 0))

    def page_spec(j):
        return pl.BlockSpec((None, None, prow, HEAD_V),
                            lambda b, s, pt: (layer_slot, pt[b * n_pages + s * pps + j], 0, 0))

    grid_spec = pltpu.PrefetchScalarGridSpec(
        num_scalar_prefetch=1,
        grid=(db, n_pages // pps),
        in_specs=[tok, tok, tok,
                  pl.BlockSpec((4, HEAD_DIM), lambda b, s, pt: (0, 0)),
                  pl.BlockSpec((1, HEAD_V), lambda b, s, pt: (0, 0))]
                 + [page_spec(j) for j in range(pps)] * 2,
        out_specs=pl.BlockSpec((None, N_HEADS_A, HEAD_V), lambda b, s, pt: (b, 0, 0)),
        scratch_shapes=[pltpu.VMEM((rows, 1), F32), pltpu.VMEM((rows, 1), F32),
                        pltpu.VMEM((rows, HEAD_V), F32), pltpu.VMEM((rows, HEAD_V), F32)],
    )
    out = pl.pallas_call(
        kernel,
        grid_spec=grid_spec,
        out_shape=jax.ShapeDtypeStruct((db, N_HEADS_A, HEAD_V), F32),
        compiler_params=_params(("parallel", "arbitrary")),
        name="decode_attn",
    )(page_table.reshape(-1), q8, k8, v8, lam_p, og, *([cache_k] * pps), *([cache_v] * pps))
    return out.reshape(db, D_A)


def _conv_tail(conv, lng, lnb, o, x, w_ref, hp=False):
    y = _layer_norm(conv, lng, lnb)
    c = y * jax.nn.sigmoid(y)
    return (x + _mm(o, w_ref[0:D_A, :], hp) + _mm(c, w_ref[D_A:D_A + D_B, :], hp))


def _even_out_kernel(hb_ref, hprev_ref, o_ref, x_ref, wdw_ref, bdw_ref, lng_ref, lnb_ref,
                     w_ref, y_ref, win_ref, conv_ref, *, tm, tiles_per_seq):
    i = pl.program_id(0)
    first = (i % tiles_per_seq) == 0
    win_ref[0, 0:CONV_HALO, :] = jnp.where(first, 0.0, hprev_ref[...])
    win_ref[0, CONV_HALO:CONV_HALO + tm, :] = hb_ref[...]
    lead = CONV_HALO - (CONV_WIDTH - 1)
    shifted_rows = tm + CONV_HALO - SUBLANES
    for s in range(1, SUBLANES):
        win_ref[s, 0:shifted_rows, :] = win_ref[0, s:s + shifted_rows, :]
    bias = bdw_ref[...]
    for r0 in range(0, tm, CONV_ROWS):
        acc = jnp.broadcast_to(bias, (CONV_ROWS, D_B))
        for j in range(CONV_WIDTH):
            s = (lead + j) % SUBLANES
            a = r0 + lead + j - s
            acc = acc + win_ref[s, a:a + CONV_ROWS, :] * wdw_ref[j:j + 1, :]
        conv_ref[r0:r0 + CONV_ROWS, :] = acc
    y_ref[...] = _conv_tail(conv_ref[...], lng_ref[...], lnb_ref[...], o_ref[...], x_ref[...],
                            w_ref)


def _even_out(hb, o, x, wdw, bdw, lng, lnb, w_out, tm, seq):
    T = x.shape[0]
    tiles_per_seq = seq // tm
    halo_blocks = tm // CONV_HALO
    row = lambda i: (i, 0)
    fix = lambda i: (0, 0)
    kernel = functools.partial(_even_out_kernel, tm=tm, tiles_per_seq=tiles_per_seq)
    return pl.pallas_call(
        kernel,
        grid=(T // tm,),
        in_specs=[pl.BlockSpec((tm, D_B), row),
                  pl.BlockSpec((CONV_HALO, D_B), lambda i: (jnp.maximum(i * halo_blocks - 1, 0), 0)),
                  pl.BlockSpec((tm, D_A), row),
                  pl.BlockSpec((tm, D_MODEL), row),
                  pl.BlockSpec((CONV_HALO, D_B), fix),
                  pl.BlockSpec((1, D_B), fix),
                  pl.BlockSpec((1, D_B), fix),
                  pl.BlockSpec((1, D_B), fix),
                  pl.BlockSpec((D_A + D_B, D_MODEL), fix)],
        out_specs=pl.BlockSpec((tm, D_MODEL), row),
        out_shape=jax.ShapeDtypeStruct((T, D_MODEL), F32),
        scratch_shapes=[pltpu.VMEM((SUBLANES, CONV_HALO + tm, D_B), F32),
                        pltpu.VMEM((tm, D_B), F32)],
        compiler_params=_params(("parallel",)),
        name="even_out",
    )(hb, hb, o, x, wdw, bdw, lng, lnb, w_out)


def _even_out_sample_kernel(st_ref, hb_ref, o_ref, x_ref, wdw_ref, bdw_ref, lng_ref, lnb_ref,
                            w_ref, y_ref, *, hp):
    w = wdw_ref[...]
    conv = jnp.sum(st_ref[...] * w[0:CONV_WIDTH - 1, :][None], axis=1)
    conv = conv + hb_ref[...] * w[CONV_WIDTH - 1:CONV_WIDTH, :] + bdw_ref[...]
    y_ref[...] = _conv_tail(conv, lng_ref[...], lnb_ref[...], o_ref[...], x_ref[...], w_ref,
                            hp)


def _even_out_sample(state, hb, o, x, wdw, bdw, lng, lnb, w_out, hp):
    db = x.shape[0]
    return pl.pallas_call(
        functools.partial(_even_out_sample_kernel, hp=hp),
        out_shape=jax.ShapeDtypeStruct((db, D_MODEL), F32),
        compiler_params=pltpu.CompilerParams(vmem_limit_bytes=VMEM_LIMIT),
        name="even_out_sample",
    )(state, hb, o, x, wdw, bdw, lng, lnb, w_out)


def _odd_front(x_ref, g_ref, wuv_ref, lng_ref, lnb_ref, hp=False):
    h = _rms(x_ref[...], g_ref[...])
    z = _mm(h, wuv_ref[...], hp)
    z = 0.5 * z * (1.0 + lax.erf(z * (1.0 / math.sqrt(2.0))))
    u = z[:, 0:D_C]
    v = _layer_norm(z[:, D_C:2 * D_C], lng_ref[...], lnb_ref[...])
    return u, v


def _odd_kernel(x_ref, g_ref, wuv_ref, lng_ref, lnb_ref, ws_ref, bs_ref, wo_ref, y_ref,
                sv_ref, *, tm):
    u, v = _odd_front(x_ref, g_ref, wuv_ref, lng_ref, lnb_ref)
    vb = v.astype(BF16)
    r = lax.broadcasted_iota(jnp.int32, (CHUNK, CHUNK), 0)
    c = lax.broadcasted_iota(jnp.int32, (CHUNK, CHUNK), 1)
    gw = D_C // N_GROUPS_C
    for g in range(N_GROUPS_C):
        wsg = jnp.where(c <= r, ws_ref[g], 0.0).astype(BF16)
        for ch in range(tm // CHUNK):
            rows = slice(ch * CHUNK, (ch + 1) * CHUNK)
            cols = slice(g * gw, (g + 1) * gw)
            sv_ref[rows, cols] = _dot(wsg, vb[rows, cols]) + bs_ref[:, cols]
    y_ref[...] = x_ref[...] + _dot((u * sv_ref[...]).astype(BF16), wo_ref[...])


def _odd(x, g, wuv, lng, lnb, ws, bs, wo, tm):
    T = x.shape[0]
    row = lambda i: (i, 0)
    fix = lambda i: (0, 0)
    return pl.pallas_call(
        functools.partial(_odd_kernel, tm=tm),
        grid=(T // tm,),
        in_specs=[pl.BlockSpec((tm, D_MODEL), row),
                  pl.BlockSpec((1, D_MODEL), fix),
                  pl.BlockSpec((D_MODEL, 2 * D_C), fix),
                  pl.BlockSpec((1, D_C), fix),
                  pl.BlockSpec((1, D_C), fix),
                  pl.BlockSpec((N_GROUPS_C, CHUNK, CHUNK), lambda i: (0, 0, 0)),
                  pl.BlockSpec((CHUNK, D_C), fix),
                  pl.BlockSpec((D_C, D_MODEL), fix)],
        out_specs=pl.BlockSpec((tm, D_MODEL), row),
        out_shape=jax.ShapeDtypeStruct((T, D_MODEL), F32),
        scratch_shapes=[pltpu.VMEM((tm, D_C), F32)],
        compiler_params=_params(("parallel",)),
        name="odd_mixer",
    )(x, g, wuv, lng, lnb, ws, bs, wo)


def _odd_sample_kernel(x_ref, g_ref, wuv_ref, lng_ref, lnb_ref, ws0_ref, bs0_ref, wo_ref,
                       y_ref, v_ref, *, hp):
    u, v = _odd_front(x_ref, g_ref, wuv_ref, lng_ref, lnb_ref, hp)
    v_ref[...] = v
    sv = v * ws0_ref[...] + bs0_ref[...]
    y_ref[...] = x_ref[...] + _mm(u * sv, wo_ref[...], hp)


def _odd_sample(x, g, wuv, lng, lnb, ws0, bs0, wo, hp):
    db = x.shape[0]
    return pl.pallas_call(
        functools.partial(_odd_sample_kernel, hp=hp),
        out_shape=[jax.ShapeDtypeStruct((db, D_MODEL), F32),
                   jax.ShapeDtypeStruct((db, D_C), F32)],
        compiler_params=pltpu.CompilerParams(vmem_limit_bytes=VMEM_LIMIT),
        name="odd_mixer_sample",
    )(x, g, wuv, lng, lnb, ws0, bs0, wo)


def _route(logits):
    lane = lax.broadcasted_iota(jnp.int32, logits.shape, 1)
    lane_f = lane.astype(F32)
    none = float(ROUTER_LANES)
    is_g = lane < N_EXPERT_GROUPS
    gl = jnp.where(is_g, logits, NEG_BIG)
    gmax = jnp.max(gl, axis=1, keepdims=True)
    gidx = jnp.min(jnp.where(is_g & (gl == gmax), lane_f, none), axis=1, keepdims=True)
    gw = 1.0 / jnp.sum(jnp.where(is_g, jnp.exp(gl - gmax), 0.0), axis=1, keepdims=True)
    e_f = lane_f - float(N_EXPERT_GROUPS)
    is_e = (lane >= N_EXPERT_GROUPS) & (lane < N_EXPERT_GROUPS + N_EXPERTS)
    grp_lo = gidx * float(EXPERTS_PER_GROUP)
    in_grp = is_e & (e_f >= grp_lo) & (e_f < grp_lo + float(EXPERTS_PER_GROUP))
    cand = jnp.where(in_grp, logits, NEG_BIG)
    v1 = jnp.max(cand, axis=1, keepdims=True)
    i1 = jnp.min(jnp.where(in_grp & (cand == v1), lane_f, none), axis=1, keepdims=True)
    rest = in_grp & (lane_f != i1)
    cand2 = jnp.where(rest, logits, NEG_BIG)
    v2 = jnp.max(cand2, axis=1, keepdims=True)
    i2 = jnp.min(jnp.where(rest & (cand2 == v2), lane_f, none), axis=1, keepdims=True)
    t = jnp.exp(v2 - v1)
    p1 = 1.0 / (1.0 + t)
    p2 = t * p1
    return jnp.where(lane_f == i1, p1 * gw, jnp.where(lane_f == i2, p2 * gw, 0.0))


def _moe_kernel(x_ref, g_ref, wr_ref, br_ref, wup_ref, wdn_ref, y_ref, xn_ref, gate_ref,
                acc_ref, *, hp):
    grp = pl.program_id(1)

    @pl.when(grp == 0)
    def _():
        xn = _rms(x_ref[...], g_ref[...]).astype(xn_ref.dtype)
        xn_ref[...] = xn
        gate_ref[...] = _route(_mm(xn, wr_ref[...], hp) + br_ref[...])
        acc_ref[...] = jnp.zeros(acc_ref.shape, F32)

    xn = xn_ref[...]
    gate = gate_ref[...]
    lane = lax.broadcasted_iota(jnp.int32, gate.shape, 1)
    acc = acc_ref[...]
    for j in range(EXPERTS_PER_GROUP):
        e_lane = N_EXPERT_GROUPS + grp * EXPERTS_PER_GROUP + j
        ge = jnp.sum(jnp.where(lane == e_lane, gate, 0.0), axis=1, keepdims=True)
        hu = _mm(xn, wup_ref[j], hp)
        a = hu[:, 0:D_FF_EXPERT]
        b = hu[:, D_FF_EXPERT:]
        act = a * jax.nn.sigmoid(a) * b * ge
        acc = acc + _mm(act, wdn_ref[j], hp)
    acc_ref[...] = acc

    @pl.when(grp == pl.num_programs(1) - 1)
    def _():
        y_ref[...] = x_ref[...] + acc_ref[...]


def _moe(x, g, wr, br, wup, wdn, layer, tm, hp=False):
    T = x.shape[0]
    row = lambda i, e: (i, 0)
    fix = lambda i, e: (0, 0)
    return pl.pallas_call(
        functools.partial(_moe_kernel, hp=hp),
        grid=(T // tm, N_EXPERT_GROUPS),
        in_specs=[pl.BlockSpec((tm, D_MODEL), row),
                  pl.BlockSpec((1, D_MODEL), fix),
                  pl.BlockSpec((D_MODEL, ROUTER_LANES), fix),
                  pl.BlockSpec((1, ROUTER_LANES), fix),
                  pl.BlockSpec((None, EXPERTS_PER_GROUP, D_MODEL, 2 * D_FF_EXPERT),
                               lambda i, e: (layer, e, 0, 0)),
                  pl.BlockSpec((None, EXPERTS_PER_GROUP, D_FF_EXPERT, D_MODEL),
                               lambda i, e: (layer, e, 0, 0))],
        out_specs=pl.BlockSpec((tm, D_MODEL), row),
        out_shape=jax.ShapeDtypeStruct((T, D_MODEL), F32),
        scratch_shapes=[pltpu.VMEM((tm, D_MODEL), F32 if hp else BF16),
                        pltpu.VMEM((tm, ROUTER_LANES), F32),
                        pltpu.VMEM((tm, D_MODEL), F32)],
        compiler_params=_params(("parallel", "arbitrary")),
        name="hier_moe",
    )(x, g, wr, br, wup, wdn)


def _rope_tables(pos):
    half = HEAD_DIM // 2
    inv = ROPE_THETA ** (-jnp.arange(half, dtype=F32) / half)
    ang = pos.astype(F32)[:, None] * inv[None, :]
    cos, sin = jnp.cos(ang), jnp.sin(ang)
    return (jnp.concatenate([cos, cos, cos, cos], axis=1),
            jnp.concatenate([-sin, sin, -sin, sin], axis=1))


def _tile_gain(g, reps):
    return jnp.tile(g.astype(F32), reps)[None, :]


def kernel(x_prompt, x_sample, cache_k, cache_v, state_conv, page_table, norm_mix, norm_ffn, w_in_even, w_out_even, q_norm, k_norm, lambda_qk, head_norm, w_dw, b_dw, conv_ln_g, conv_ln_b, w_uv, v_ln_g, v_ln_b, w_spatial, b_spatial, w_out_odd, w_router_group, b_router_group, w_router_expert, b_router_expert, w_expert_up, w_expert_down):
    batch, seq, _ = x_prompt.shape
    db = x_sample.shape[0]
    n_pages = page_table.shape[1]
    page = cache_k.shape[2]
    past_len = n_pages * page
    tm = TILE_TOKENS

    xp = x_prompt.reshape(batch * seq, D_MODEL)
    xs = x_sample.reshape(db, D_MODEL)
    cos_p, sin_p = _rope_tables(jnp.arange(seq))
    cos_s, sin_s = _rope_tables(jnp.full((db,), past_len))
    ck = cache_k.reshape(cache_k.shape[0], cache_k.shape[1], page * N_HEADS_A, HEAD_V)
    cv = cache_v.reshape(cache_v.shape[0], cache_v.shape[1], page * N_HEADS_A, HEAD_V)
    group_of = jnp.arange(D_A) // HEAD_DIM
    gmat32 = jnp.where(group_of[:, None] == group_of[None, :], 1.0 / HEAD_DIM, 0.0).astype(F32)
    gmat = gmat32.astype(BF16)
    w_in_bf = w_in_even.astype(BF16)
    wup_bf = w_expert_up.astype(BF16)
    wdn_bf = w_expert_down.astype(BF16)

    kp_l, vp_l, ks_l, vs_l, cp_l, cs_l, chs_l = [], [], [], [], [], [], []
    for layer in range(DEPTH):
        i = layer // 2
        g_mix = norm_mix[layer][None, :]
        hp = layer < HP_SAMPLE_LAYERS
        if layer % 2 == 0:
            lam_init = 0.8 - 0.6 * math.exp(-0.3 * layer)
            w_out = w_out_even[i].astype(BF16)
            qg = _tile_gain(q_norm[i], D_A // HEAD_DIM)
            kg = _tile_gain(k_norm[i], D_A // HEAD_DIM)
            og = head_norm[i][None, :]
            wdw = jnp.pad(w_dw[i], ((0, CONV_HALO - CONV_WIDTH), (0, 0)))
            bdw, lng, lnb = b_dw[i][None, :], conv_ln_g[i][None, :], conv_ln_b[i][None, :]
            q, k32, kb, v32, vb, hb = _even_in(xp, g_mix, w_in_bf, i, qg, kg, gmat, cos_p, sin_p,
                                               tm, seq // tm)
            o = _prompt_attention(q, kb, vb, lambda_qk[i], og, batch, seq, lam_init)
            xp = _even_out(hb, o, xp, wdw, bdw, lng, lnb, w_out, tm, seq)
            kp_l.append(k32.reshape(batch, seq, N_HEADS_A, HEAD_V))
            vp_l.append(v32.reshape(batch, seq, N_HEADS_A, HEAD_V))
            cp_l.append(hb.reshape(batch, seq, D_B)[:, seq - (CONV_WIDTH - 1):, :])
            qs, ks32, _, vs32, _, hbs = _even_in(
                xs, g_mix, w_in_even if hp else w_in_bf, i, qg, kg, gmat32 if hp else gmat,
                cos_s, sin_s, db, 1, hp)
            os_ = _decode_attention(page_table, qs, ks32, vs32, lambda_qk[i], og, ck, cv, i,
                                    lam_init, hp)
            xs = _even_out_sample(state_conv[i], hbs, os_, xs, wdw, bdw, lng,
                                  lnb, w_out_even[i] if hp else w_out, hp)
            ks_l.append(ks32.reshape(db, 1, N_HEADS_A, HEAD_V))
            vs_l.append(vs32.reshape(db, 1, N_HEADS_A, HEAD_V))
            cs_l.append(jnp.concatenate([state_conv[i][:, 1:, :], hbs[:, None, :]], axis=1))
        else:
            wuv = w_uv[i].astype(BF16)
            wo = w_out_odd[i].astype(BF16)
            lng, lnb = v_ln_g[i][None, :], v_ln_b[i][None, :]
            gw = D_C // N_GROUPS_C
            bs = jnp.repeat(b_spatial[i].T, gw, axis=1)
            xp = _odd(xp, g_mix, wuv, lng, lnb, w_spatial[i], bs, wo, tm)
            ws0 = jnp.repeat(w_spatial[i][:, 0, 0], gw)[None, :]
            xs, vch = _odd_sample(xs, g_mix, w_uv[i] if hp else wuv, lng, lnb, ws0, bs[0:1, :],
                                  w_out_odd[i] if hp else wo, hp)
            chs_l.append(vch.reshape(db, 1, D_C))
        g_ffn = norm_ffn[layer][None, :]
        pad = ROUTER_LANES - N_EXPERT_GROUPS - N_EXPERTS
        wr32 = jnp.pad(jnp.concatenate([w_router_group[layer], w_router_expert[layer]], axis=1),
                       ((0, 0), (0, pad)))
        wr = wr32.astype(BF16)
        br = jnp.pad(jnp.concatenate([b_router_group[layer], b_router_expert[layer]]),
                     (0, pad))[None, :]
        xp = _moe(xp, g_ffn, wr, br, wup_bf, wdn_bf, layer, MOE_TILE_TOKENS)
        if hp:
            xs = _moe(xs, g_ffn, wr32, br, w_expert_up, w_expert_down, layer, db, True)
        else:
            xs = _moe(xs, g_ffn, wr, br, wup_bf, wdn_bf, layer, db)

    return (xp.reshape(batch, seq, D_MODEL), xs.reshape(db, 1, D_MODEL),
            jnp.stack(kp_l), jnp.stack(vp_l), jnp.stack(ks_l), jnp.stack(vs_l),
            jnp.stack(cp_l), jnp.stack(cs_l), jnp.stack(chs_l))
```

```python
import functools
import math

import jax
import jax.numpy as jnp
from jax import lax
from jax.experimental import pallas as pl
from jax.experimental.pallas import tpu as pltpu

F32 = jnp.float32
BF16 = jnp.bfloat16

D_MODEL = 1024
DEPTH = 4
N_HEADS_A = 4
HEAD_DIM = 64
HEAD_V = 2 * HEAD_DIM
D_A = N_HEADS_A * HEAD_V
D_B = D_MODEL // 2
CONV_WIDTH = 31
CONV_HALO = 32
D_IN_EVEN = 3 * D_A + 2 * D_B
D_C = D_MODEL
N_GROUPS_C = 4
CHUNK = 128
N_EXPERT_GROUPS = 4
EXPERTS_PER_GROUP = 4
N_EXPERTS = N_EXPERT_GROUPS * EXPERTS_PER_GROUP
D_FF_EXPERT = 256
ROPE_THETA = 10000.0
EPS = 1e-6
NEG_BIG = -1e30
ROUTER_LANES = 128
SUBLANES = 8

VMEM_LIMIT = 52 * 1024 * 1024

TILE_TOKENS = 512
EVEN_IN_SPLIT = 2
ODD_SPLIT = 1
MOE_TILE_TOKENS = 1024
ATTN_BLOCK = 512
ATTN_SUB = 256
CONV_ROWS = 64
PAGES_PER_STEP = 32
HP_SAMPLE_LAYERS = 2


def _params(sem):
    return pltpu.CompilerParams(dimension_semantics=sem, vmem_limit_bytes=VMEM_LIMIT)


def _rms(x, g):
    return x * lax.rsqrt(jnp.mean(x * x, axis=-1, keepdims=True) + EPS) * g


def _layer_norm(x, g, b):
    mu = jnp.mean(x, axis=-1, keepdims=True)
    xc = x - mu
    var = jnp.mean(xc * xc, axis=-1, keepdims=True)
    return xc * lax.rsqrt(var + EPS) * g + b


def _dot(a, b):
    return jnp.dot(a, b, preferred_element_type=F32)


def _dot_nt(a, b):
    return lax.dot_general(a, b, (((1,), (1,)), ((), ())), preferred_element_type=F32)


def _split(x):
    hi = x.astype(BF16)
    return hi, (x - hi.astype(F32)).astype(BF16)


def _stack_split(x):
    hi = x.astype(BF16).astype(F32)
    return jnp.concatenate([hi, x - hi], axis=0).astype(BF16)


def _mm(a, w, hp):
    if not hp:
        return _dot(a.astype(BF16), w)
    m = a.shape[0]
    wh, wl = _split(w)
    top = _dot(_stack_split(a), wh)
    return top[0:m] + top[m:2 * m] + _dot(a.astype(BF16), wl)


def _lambda(lp, lam_init):
    a = jnp.sum(lp[0:1] * lp[1:2], axis=1, keepdims=True)
    b = jnp.sum(lp[2:3] * lp[3:4], axis=1, keepdims=True)
    return jnp.exp(a) - jnp.exp(b) + lam_init


def _even_in_kernel(x_ref, g_ref, w_ref, qg_ref, kg_ref, gmat_ref, cos_ref, sin_ref, *rest,
                    hp, slot, first):
    q_ref, k32_ref, kb_ref, v32_ref, vb_ref, hb_ref = rest[-6:]
    tm = x_ref.shape[0]
    if first:
        for other in range(k32_ref.shape[0]):
            if other != slot:
                k32_ref[other] = jnp.zeros(k32_ref.shape[1:], F32)
                v32_ref[other] = jnp.zeros(v32_ref.shape[1:], F32)
        k_out, v_out = k32_ref.at[slot], v32_ref.at[slot]
    else:
        k_out, v_out = k32_ref, v32_ref
    lane = lax.broadcasted_iota(jnp.int32, (tm // EVEN_IN_SPLIT, D_A), 1)
    first_half = (lane % HEAD_DIM) < (HEAD_DIM // 2)

    for part in range(EVEN_IN_SPLIT):
        rows = tm // EVEN_IN_SPLIT
        r0 = part * rows
        sl = slice(r0, r0 + rows)
        h = _rms(x_ref[sl, :], g_ref[...])
        proj = _mm(h, w_ref[...], hp)
        cos = jnp.concatenate([cos_ref[sl, :]] * (D_A // 128), axis=1)
        sin = jnp.concatenate([sin_ref[sl, :]] * (D_A // 128), axis=1)

        def norm_rope(z, gain):
            ms = _mm(z * z, gmat_ref[...], hp)
            zn = z * lax.rsqrt(ms + EPS) * gain
            rot = jnp.where(first_half,
                            pltpu.roll(zn, D_A - HEAD_DIM // 2, 1),
                            pltpu.roll(zn, HEAD_DIM // 2, 1))
            return zn * cos + rot * sin

        q = norm_rope(proj[:, 0:D_A], qg_ref[...])
        k = norm_rope(proj[:, D_A:2 * D_A], kg_ref[...])
        v = proj[:, 2 * D_A:3 * D_A]
        a = proj[:, 3 * D_A:3 * D_A + D_B]
        g = proj[:, 3 * D_A + D_B:]
        q_ref[sl, :] = (q * (1.0 / math.sqrt(HEAD_DIM))).astype(q_ref.dtype)
        kb_ref[sl, :] = k.astype(BF16)
        vb_ref[sl, :] = v.astype(BF16)
        hb_ref[sl, :] = a * jax.nn.sigmoid(g)
        for hd in range(N_HEADS_A):
            cols = slice(hd * HEAD_V, (hd + 1) * HEAD_V)
            dst = pl.ds(r0 * N_HEADS_A + hd, rows, stride=N_HEADS_A)
            k_out[dst, :] = k[:, cols]
            v_out[dst, :] = v[:, cols]


def _even_in(x, g, w_in, slot, n_slots, kv_prev, qg, kg, gmat, cos, sin, tm, pos_blocks, hp=False):
    T = x.shape[0]
    row = lambda i: (i, 0)
    fix = lambda i: (0, 0)
    pos = lambda i: (i % pos_blocks, 0)
    stacked = jax.ShapeDtypeStruct((n_slots, T * N_HEADS_A, HEAD_V), F32)
    outs = [jax.ShapeDtypeStruct((T, D_A), F32 if hp else BF16),
            stacked,
            jax.ShapeDtypeStruct((T, D_A), BF16),
            stacked,
            jax.ShapeDtypeStruct((T, D_A), BF16),
            jax.ShapeDtypeStruct((T, D_B), F32)]
    wide = pl.BlockSpec((tm, D_A), row)
    first = kv_prev is None
    if first:
        tall = pl.BlockSpec((n_slots, tm * N_HEADS_A, HEAD_V), lambda i: (0, i, 0))
        extra_specs, extra_args, aliases = [], [], {}
    else:
        tall = pl.BlockSpec((None, tm * N_HEADS_A, HEAD_V), lambda i: (slot, i, 0))
        extra_specs = [pl.BlockSpec(memory_space=pl.ANY)] * 2
        extra_args = list(kv_prev)
        aliases = {8: 1, 9: 3}
    return pl.pallas_call(
        functools.partial(_even_in_kernel, hp=hp, slot=slot, first=first),
        grid=(T // tm,),
        in_specs=[pl.BlockSpec((tm, D_MODEL), row),
                  pl.BlockSpec((1, D_MODEL), fix),
                  pl.BlockSpec((None, D_MODEL, D_IN_EVEN), lambda i: (slot, 0, 0)),
                  pl.BlockSpec((1, D_A), fix),
                  pl.BlockSpec((1, D_A), fix),
                  pl.BlockSpec((D_A, D_A), fix),
                  pl.BlockSpec((tm, 128), pos),
                  pl.BlockSpec((tm, 128), pos)] + extra_specs,
        out_specs=[wide, tall, wide, tall, wide, pl.BlockSpec((tm, D_B), row)],
        out_shape=outs,
        input_output_aliases=aliases,
        compiler_params=_params(("parallel",)),
        name="even_in",
    )(x, g, w_in, qg, kg, gmat, cos, sin, *extra_args)


def _attn_kernel(q_ref, k_ref, v_ref, lam_ref, og_ref, o_ref, v1_ref, sa_ref, sb_ref, m_ref,
                 acc_ref, *, blk, sub, lam_init):
    qi = pl.program_id(2)
    per_blk = blk // sub
    assert per_blk % 2 == 0

    @pl.when(qi == 0)
    def _():
        v1_ref[:, 0:HEAD_V] = v_ref[...]
        v1_ref[:, HEAD_V:2 * HEAD_V] = jnp.ones((v1_ref.shape[0], HEAD_V), BF16)

    q = q_ref[...]
    lane = lax.broadcasted_iota(jnp.int32, q.shape, 1)
    zero = jnp.zeros_like(q)
    qq = jnp.concatenate([jnp.where(lane < HEAD_DIM, q, zero),
                          jnp.where(lane >= HEAD_DIM, q, zero)], axis=0)
    m_ref[...] = jnp.full(m_ref.shape, NEG_BIG, F32)
    acc_ref[...] = jnp.zeros(acc_ref.shape, F32)

    def scores(start):
        if not isinstance(start, int):
            start = pl.multiple_of(start, sub)
        return _dot_nt(qq, k_ref[pl.ds(start, sub), :])

    def consume(s, start, mask):
        vblk = v1_ref[pl.ds(start, sub), :]
        if mask is not None:
            s = jnp.where(mask, s, NEG_BIG)
        m_old = m_ref[...]
        m_new = jnp.maximum(m_old, jnp.max(s, axis=1, keepdims=True))
        alpha = jnp.exp(m_old - m_new)
        p = jnp.exp((s - jnp.concatenate([m_new] * (sub // 128), axis=1)).astype(BF16))
        acc_ref[...] = (jnp.concatenate([alpha, alpha], axis=1) * acc_ref[...]
                        + _dot(p, vblk))
        m_ref[...] = m_new

    bufs = (sa_ref, sb_ref)
    sa_ref[...] = scores(0)

    def full_blocks(first, count):
        for u in range(count * per_blk):
            start = pl.multiple_of(first * blk + u * sub, sub)
            bufs[(u + 1) % 2][...] = scores(start + sub)
            consume(bufs[u % 2][...], start, None)

    def body(t, carry):
        full_blocks(2 * t, 2)
        return carry

    lax.fori_loop(0, qi // 2, body, 0)

    @pl.when(qi % 2 == 1)
    def _():
        full_blocks(qi - 1, 1)

    r = lax.broadcasted_iota(jnp.int32, (2 * blk, sub), 0)
    c = lax.broadcasted_iota(jnp.int32, (2 * blk, sub), 1)
    r = jnp.where(r >= blk, r - blk, r)
    for u in range(per_blk):
        start = pl.multiple_of(qi * blk + u * sub, sub)
        if u + 1 < per_blk:
            bufs[(u + 1) % 2][...] = scores(start + sub)
        consume(bufs[u % 2][...], start, c + u * sub <= r)

    lam = _lambda(lam_ref[...], lam_init)
    acc = acc_ref[...]
    ratio = acc[:, 0:HEAD_V] / acc[:, HEAD_V:2 * HEAD_V]
    o = ratio[0:blk] - lam * ratio[blk:2 * blk]
    o_ref[...] = (_rms(o, og_ref[...]) * (1.0 - lam_init)).astype(o_ref.dtype)


def _prompt_attention(q, kb, vb, lam_p, og, batch, seq, lam_init):
    blk = ATTN_BLOCK
    nq = seq // blk
    sub = ATTN_SUB
    kernel = functools.partial(_attn_kernel, blk=blk, sub=sub, lam_init=lam_init)
    scr = [pltpu.VMEM((seq, 2 * HEAD_V), BF16), pltpu.VMEM((2 * blk, sub), F32),
           pltpu.VMEM((2 * blk, sub), F32), pltpu.VMEM((2 * blk, HEAD_V), F32),
           pltpu.VMEM((2 * blk, 2 * HEAD_V), F32)]
    return pl.pallas_call(
        kernel,
        grid=(batch, N_HEADS_A, nq),
        in_specs=[pl.BlockSpec((blk, HEAD_V), lambda b, h, i: (b * nq + i, h)),
                  pl.BlockSpec((seq, HEAD_V), lambda b, h, i: (b, h)),
                  pl.BlockSpec((seq, HEAD_V), lambda b, h, i: (b, h)),
                  pl.BlockSpec((4, HEAD_DIM), lambda b, h, i: (0, 0)),
                  pl.BlockSpec((1, HEAD_V), lambda b, h, i: (0, 0))],
        out_specs=pl.BlockSpec((blk, HEAD_V), lambda b, h, i: (b * nq + i, h)),
        out_shape=jax.ShapeDtypeStruct((batch * seq, D_A), BF16),
        scratch_shapes=scr,
        compiler_params=_params(("parallel", "parallel", "arbitrary")),
        name="prompt_attn",
    )(q, kb, vb, lam_p, og)


def _decode_kernel(pt_ref, q_ref, kn_ref, vn_ref, lam_ref, og_ref, *rest, pps, lam_init, hp):
    k_refs = rest[:pps]
    v_refs = rest[pps:2 * pps]
    o_ref = rest[2 * pps]
    m_ref, l_ref, acc_ref, r_ref = rest[2 * pps + 1:]
    s_id = pl.program_id(1)
    rows = 2 * N_HEADS_A
    prow = k_refs[0].shape[0]

    q8 = q_ref[...]
    qh = q8.astype(BF16)
    if hp:
        q_used = q8
        q_lhs = _stack_split(q8)
    else:
        q_used = qh.astype(F32)
    row = lax.broadcasted_iota(jnp.int32, (rows, prow), 0)
    col = lax.broadcasted_iota(jnp.int32, (rows, prow), 1)
    own = (col % N_HEADS_A) == (row // 2)

    @pl.when(s_id == 0)
    def _():
        m_ref[...] = jnp.full(m_ref.shape, NEG_BIG, F32)
        l_ref[...] = jnp.zeros(l_ref.shape, F32)
        acc_ref[...] = jnp.zeros(acc_ref.shape, F32)

    def scores(k):
        if not hp:
            s = _dot_nt(qh, k.astype(BF16))
        else:
            kh, kl = _split(k)
            top = _dot_nt(q_lhs, kh)
            s = top[0:rows] + top[rows:2 * rows] + _dot_nt(qh, kl)
        return jnp.where(own, s, NEG_BIG)

    s = jnp.concatenate([scores(k_refs[j][...]) for j in range(pps)], axis=1)
    m_old = m_ref[...]
    m_new = jnp.maximum(m_old, jnp.max(s, axis=1, keepdims=True))
    alpha = jnp.exp(m_old - m_new)
    p = jnp.exp(s - m_new)
    l_ref[...] = alpha * l_ref[...] + jnp.sum(p, axis=1, keepdims=True)
    ph = p.astype(BF16)
    if hp:
        p_lhs = _stack_split(p)
    pv = jnp.zeros((rows, HEAD_V), F32)
    for j in range(pps):
        cols = slice(j * prow, (j + 1) * prow)
        v = v_refs[j][...]
        if hp:
            vh, vl = _split(v)
            top = _dot(p_lhs[:, cols], vh)
            pv = pv + top[0:rows] + top[rows:2 * rows] + _dot(ph[:, cols], vl)
        else:
            pv = pv + _dot(ph[:, cols], v.astype(BF16))
    acc_ref[...] = alpha * acc_ref[...] + pv
    m_ref[...] = m_new

    @pl.when(s_id == pl.num_programs(1) - 1)
    def _():
        s_new = jnp.sum(q_used * kn_ref[...], axis=1, keepdims=True)
        m_old = m_ref[...]
        m_fin = jnp.maximum(m_old, s_new)
        alpha = jnp.exp(m_old - m_fin)
        p_new = jnp.exp(s_new - m_fin)
        l_fin = alpha * l_ref[...] + p_new
        r_ref[...] = (alpha * acc_ref[...] + p_new * vn_ref[...]) / l_fin
        lam = _lambda(lam_ref[...], lam_init)
        o = (r_ref[pl.ds(0, N_HEADS_A, stride=2), :]
             - lam * r_ref[pl.ds(1, N_HEADS_A, stride=2), :])
        o_ref[...] = _rms(o, og_ref[...]) * (1.0 - lam_init)


def _decode_attention(page_table, q, k_new, v_new, lam_p, og, cache_k, cache_v, layer_slot,
                      lam_init, hp):
    db, n_pages = page_table.shape
    pps = PAGES_PER_STEP
    prow = cache_k.shape[2]
    rows = 2 * N_HEADS_A
    half = jnp.arange(HEAD_V) // HEAD_DIM
    map_mask = (half[None, :] == jnp.arange(2)[:, None]).astype(F32)
    q8 = (q.astype(F32).reshape(db, N_HEADS_A, 1, HEAD_V) * map_mask).reshape(db, rows, HEAD_V)
    k8 = jnp.repeat(k_new.reshape(db, N_HEADS_A, HEAD_V), 2, axis=1)
    v8 = jnp.repeat(v_new.reshape(db, N_HEADS_A, HEAD_V), 2, axis=1)
    kernel = functools.partial(_decode_kernel, pps=pps, lam_init=lam_init, hp=hp)
    tok = pl.BlockSpec((None, rows, HEAD_V), lambda b, s, pt: (b, 0, 0))

    def page_spec(j):
        return pl.BlockSpec((None, None, prow, HEAD_V),
                            lambda b, s, pt: (layer_slot, pt[b * n_pages + s * pps + j], 0, 0))

    grid_spec = pltpu.PrefetchScalarGridSpec(
        num_scalar_prefetch=1,
        grid=(db, n_pages // pps),
        in_specs=[tok, tok, tok,
                  pl.BlockSpec((4, HEAD_DIM), lambda b, s, pt: (0, 0)),
                  pl.BlockSpec((1, HEAD_V), lambda b, s, pt: (0, 0))]
                 + [page_spec(j) for j in range(pps)] * 2,
        out_specs=pl.BlockSpec((None, N_HEADS_A, HEAD_V), lambda b, s, pt: (b, 0, 0)),
        scratch_shapes=[pltpu.VMEM((rows, 1), F32), pltpu.VMEM((rows, 1), F32),
                        pltpu.VMEM((rows, HEAD_V), F32), pltpu.VMEM((rows, HEAD_V), F32)],
    )
    out = pl.pallas_call(
        kernel,
        grid_spec=grid_spec,
        out_shape=jax.ShapeDtypeStruct((db, N_HEADS_A, HEAD_V), F32),
        compiler_params=_params(("parallel", "arbitrary")),
        name="decode_attn",
    )(page_table.reshape(-1), q8, k8, v8, lam_p, og, *([cache_k] * pps), *([cache_v] * pps))
    return out.reshape(db, D_A)


def _conv_tail(conv, lng, lnb, o, x, w_ref, hp=False):
    y = _layer_norm(conv, lng, lnb)
    c = y * jax.nn.sigmoid(y)
    return (x + _mm(o, w_ref[0:D_A, :], hp) + _mm(c, w_ref[D_A:D_A + D_B, :], hp))


def _even_out_kernel(hb_ref, hprev_ref, o_ref, x_ref, wdw_ref, bdw_ref, lng_ref, lnb_ref,
                     w_ref, y_ref, win_ref, conv_ref, *, tm, tiles_per_seq):
    i = pl.program_id(0)
    first = (i % tiles_per_seq) == 0
    win_ref[0, 0:CONV_HALO, :] = jnp.where(first, 0.0, hprev_ref[...])
    win_ref[0, CONV_HALO:CONV_HALO + tm, :] = hb_ref[...]
    lead = CONV_HALO - (CONV_WIDTH - 1)
    shifted_rows = tm + CONV_HALO - SUBLANES
    for s in range(1, SUBLANES):
        win_ref[s, 0:shifted_rows, :] = win_ref[0, s:s + shifted_rows, :]
    bias = bdw_ref[...]
    for r0 in range(0, tm, CONV_ROWS):
        acc = jnp.broadcast_to(bias, (CONV_ROWS, D_B))
        for j in range(CONV_WIDTH):
            s = (lead + j) % SUBLANES
            a = r0 + lead + j - s
            acc = acc + win_ref[s, a:a + CONV_ROWS, :] * wdw_ref[j:j + 1, :]
        conv_ref[r0:r0 + CONV_ROWS, :] = acc
    y_ref[...] = _conv_tail(conv_ref[...], lng_ref[...], lnb_ref[...], o_ref[...], x_ref[...],
                            w_ref)


def _even_out(hb, o, x, wdw, bdw, lng, lnb, w_out, tm, seq):
    T = x.shape[0]
    tiles_per_seq = seq // tm
    halo_blocks = tm // CONV_HALO
    row = lambda i: (i, 0)
    fix = lambda i: (0, 0)
    kernel = functools.partial(_even_out_kernel, tm=tm, tiles_per_seq=tiles_per_seq)
    return pl.pallas_call(
        kernel,
        grid=(T // tm,),
        in_specs=[pl.BlockSpec((tm, D_B), row),
                  pl.BlockSpec((CONV_HALO, D_B), lambda i: (jnp.maximum(i * halo_blocks - 1, 0), 0)),
                  pl.BlockSpec((tm, D_A), row),
                  pl.BlockSpec((tm, D_MODEL), row),
                  pl.BlockSpec((CONV_HALO, D_B), fix),
                  pl.BlockSpec((1, D_B), fix),
                  pl.BlockSpec((1, D_B), fix),
                  pl.BlockSpec((1, D_B), fix),
                  pl.BlockSpec((D_A + D_B, D_MODEL), fix)],
        out_specs=pl.BlockSpec((tm, D_MODEL), row),
        out_shape=jax.ShapeDtypeStruct((T, D_MODEL), F32),
        scratch_shapes=[pltpu.VMEM((SUBLANES, CONV_HALO + tm, D_B), F32),
                        pltpu.VMEM((tm, D_B), F32)],
        compiler_params=_params(("parallel",)),
        name="even_out",
    )(hb, hb, o, x, wdw, bdw, lng, lnb, w_out)


def _even_out_sample_kernel(st_ref, hb_ref, o_ref, x_ref, wdw_ref, bdw_ref, lng_ref, lnb_ref,
                            w_ref, y_ref, *, hp):
    w = wdw_ref[...]
    conv = jnp.sum(st_ref[...] * w[0:CONV_WIDTH - 1, :][None], axis=1)
    conv = conv + hb_ref[...] * w[CONV_WIDTH - 1:CONV_WIDTH, :] + bdw_ref[...]
    y_ref[...] = _conv_tail(conv, lng_ref[...], lnb_ref[...], o_ref[...], x_ref[...], w_ref,
                            hp)


def _even_out_sample(state, hb, o, x, wdw, bdw, lng, lnb, w_out, hp):
    db = x.shape[0]
    return pl.pallas_call(
        functools.partial(_even_out_sample_kernel, hp=hp),
        out_shape=jax.ShapeDtypeStruct((db, D_MODEL), F32),
        compiler_params=pltpu.CompilerParams(vmem_limit_bytes=VMEM_LIMIT),
        name="even_out_sample",
    )(state, hb, o, x, wdw, bdw, lng, lnb, w_out)


def _odd_front(x, g_ref, wuv_ref, lng_ref, lnb_ref, hp=False):
    h = _rms(x, g_ref[...])
    z = _mm(h, wuv_ref[...], hp)
    z = 0.5 * z * (1.0 + lax.erf(z * (1.0 / math.sqrt(2.0))))
    u = z[:, 0:D_C]
    v = _layer_norm(z[:, D_C:2 * D_C], lng_ref[...], lnb_ref[...])
    return u, v


def _odd_kernel(x_ref, g_ref, wuv_ref, lng_ref, lnb_ref, ws_ref, bs_ref, wo_ref, y_ref,
                sv_ref, *, tm):
    r = lax.broadcasted_iota(jnp.int32, (CHUNK, CHUNK), 0)
    c = lax.broadcasted_iota(jnp.int32, (CHUNK, CHUNK), 1)
    gw = D_C // N_GROUPS_C
    wsg = [jnp.where(c <= r, ws_ref[g], 0.0).astype(BF16)
           for g in range(N_GROUPS_C)]
    slab = tm // ODD_SPLIT
    for part in range(ODD_SPLIT):
        sl = slice(part * slab, (part + 1) * slab)
        x = x_ref[sl, :]
        u, v = _odd_front(x, g_ref, wuv_ref, lng_ref, lnb_ref)
        vb = v.astype(BF16)
        for g in range(N_GROUPS_C):
            cols = slice(g * gw, (g + 1) * gw)
            for ch in range(slab // CHUNK):
                rows = slice(ch * CHUNK, (ch + 1) * CHUNK)
                dst = slice(part * slab + ch * CHUNK, part * slab + (ch + 1) * CHUNK)
                sv_ref[dst, cols] = _dot(wsg[g], vb[rows, cols]) + bs_ref[:, cols]
        y_ref[sl, :] = x + _dot((u * sv_ref[sl, :]).astype(BF16), wo_ref[...])


def _odd(x, g, wuv, lng, lnb, ws, bs, wo, tm):
    T = x.shape[0]
    row = lambda i: (i, 0)
    fix = lambda i: (0, 0)
    return pl.pallas_call(
        functools.partial(_odd_kernel, tm=tm),
        grid=(T // tm,),
        in_specs=[pl.BlockSpec((tm, D_MODEL), row),
                  pl.BlockSpec((1, D_MODEL), fix),
                  pl.BlockSpec((D_MODEL, 2 * D_C), fix),
                  pl.BlockSpec((1, D_C), fix),
                  pl.BlockSpec((1, D_C), fix),
                  pl.BlockSpec((N_GROUPS_C, CHUNK, CHUNK), lambda i: (0, 0, 0)),
                  pl.BlockSpec((CHUNK, D_C), fix),
                  pl.BlockSpec((D_C, D_MODEL), fix)],
        out_specs=pl.BlockSpec((tm, D_MODEL), row),
        out_shape=jax.ShapeDtypeStruct((T, D_MODEL), F32),
        scratch_shapes=[pltpu.VMEM((tm, D_C), F32)],
        compiler_params=_params(("parallel",)),
        name="odd_mixer",
    )(x, g, wuv, lng, lnb, ws, bs, wo)


def _odd_sample_kernel(x_ref, g_ref, wuv_ref, lng_ref, lnb_ref, ws0_ref, bs0_ref, wo_ref,
                       y_ref, v_ref, *, hp):
    u, v = _odd_front(x_ref[...], g_ref, wuv_ref, lng_ref, lnb_ref, hp)
    v_ref[...] = v
    sv = v * ws0_ref[...] + bs0_ref[...]
    y_ref[...] = x_ref[...] + _mm(u * sv, wo_ref[...], hp)


def _odd_sample(x, g, wuv, lng, lnb, ws0, bs0, wo, hp):
    db = x.shape[0]
    return pl.pallas_call(
        functools.partial(_odd_sample_kernel, hp=hp),
        out_shape=[jax.ShapeDtypeStruct((db, D_MODEL), F32),
                   jax.ShapeDtypeStruct((db, D_C), F32)],
        compiler_params=pltpu.CompilerParams(vmem_limit_bytes=VMEM_LIMIT),
        name="odd_mixer_sample",
    )(x, g, wuv, lng, lnb, ws0, bs0, wo)


def _route(logits):
    lane = lax.broadcasted_iota(jnp.int32, logits.shape, 1)
    lane_f = lane.astype(F32)
    none = float(ROUTER_LANES)
    is_g = lane < N_EXPERT_GROUPS
    gl = jnp.where(is_g, logits, NEG_BIG)
    gmax = jnp.max(gl, axis=1, keepdims=True)
    gidx = jnp.min(jnp.where(is_g & (gl == gmax), lane_f, none), axis=1, keepdims=True)
    gw = 1.0 / jnp.sum(jnp.where(is_g, jnp.exp(gl - gmax), 0.0), axis=1, keepdims=True)
    e_f = lane_f - float(N_EXPERT_GROUPS)
    is_e = (lane >= N_EXPERT_GROUPS) & (lane < N_EXPERT_GROUPS + N_EXPERTS)
    grp_lo = gidx * float(EXPERTS_PER_GROUP)
    in_grp = is_e & (e_f >= grp_lo) & (e_f < grp_lo + float(EXPERTS_PER_GROUP))
    cand = jnp.where(in_grp, logits, NEG_BIG)
    v1 = jnp.max(cand, axis=1, keepdims=True)
    i1 = jnp.min(jnp.where(in_grp & (cand == v1), lane_f, none), axis=1, keepdims=True)
    rest = in_grp & (lane_f != i1)
    cand2 = jnp.where(rest, logits, NEG_BIG)
    v2 = jnp.max(cand2, axis=1, keepdims=True)
    i2 = jnp.min(jnp.where(rest & (cand2 == v2), lane_f, none), axis=1, keepdims=True)
    t = jnp.exp(v2 - v1)
    p1 = 1.0 / (1.0 + t)
    p2 = t * p1
    return jnp.where(lane_f == i1, p1 * gw, jnp.where(lane_f == i2, p2 * gw, 0.0))


def _moe_kernel(x_ref, g_ref, wr_ref, br_ref, wup_ref, wdn_ref, y_ref, xn_ref, gate_ref,
                acc_ref, *, hp):
    grp = pl.program_id(1)

    def experts(xn, gate, group):
        lane = lax.broadcasted_iota(jnp.int32, gate.shape, 1)
        out = None
        for j in range(EXPERTS_PER_GROUP):
            e_lane = N_EXPERT_GROUPS + group * EXPERTS_PER_GROUP + j
            ge = jnp.sum(jnp.where(lane == e_lane, gate, 0.0), axis=1, keepdims=True)
            hu = _mm(xn, wup_ref[j], hp)
            a = hu[:, 0:D_FF_EXPERT]
            b = hu[:, D_FF_EXPERT:]
            act = a * jax.nn.sigmoid(a) * b * ge
            y = _mm(act, wdn_ref[j], hp)
            out = y if out is None else out + y
        return out

    @pl.when(grp == 0)
    def _():
        xn = _rms(x_ref[...], g_ref[...]).astype(xn_ref.dtype)
        xn_ref[...] = xn
        gate = _route(_mm(xn, wr_ref[...], hp) + br_ref[...])
        gate_ref[...] = gate
        acc_ref[...] = experts(xn, gate, 0)

    @pl.when(grp > 0)
    def _():
        acc_ref[...] = acc_ref[...] + experts(xn_ref[...], gate_ref[...], grp)

    @pl.when(grp == pl.num_programs(1) - 1)
    def _():
        y_ref[...] = x_ref[...] + acc_ref[...]


def _moe(x, g, wr, br, wup, wdn, layer, tm, hp=False):
    T = x.shape[0]
    row = lambda i, e: (i, 0)
    fix = lambda i, e: (0, 0)
    return pl.pallas_call(
        functools.partial(_moe_kernel, hp=hp),
        grid=(T // tm, N_EXPERT_GROUPS),
        in_specs=[pl.BlockSpec((tm, D_MODEL), row),
                  pl.BlockSpec((1, D_MODEL), fix),
                  pl.BlockSpec((D_MODEL, ROUTER_LANES), fix),
                  pl.BlockSpec((1, ROUTER_LANES), fix),
                  pl.BlockSpec((None, EXPERTS_PER_GROUP, D_MODEL, 2 * D_FF_EXPERT),
                               lambda i, e: (layer, e, 0, 0)),
                  pl.BlockSpec((None, EXPERTS_PER_GROUP, D_FF_EXPERT, D_MODEL),
                               lambda i, e: (layer, e, 0, 0))],
        out_specs=pl.BlockSpec((tm, D_MODEL), row),
        out_shape=jax.ShapeDtypeStruct((T, D_MODEL), F32),
        scratch_shapes=[pltpu.VMEM((tm, D_MODEL), F32 if hp else BF16),
                        pltpu.VMEM((tm, ROUTER_LANES), F32),
                        pltpu.VMEM((tm, D_MODEL), F32)],
        compiler_params=_params(("parallel", "arbitrary")),
        name="hier_moe",
    )(x, g, wr, br, wup, wdn)


def _rope_tables(pos):
    half = HEAD_DIM // 2
    inv = ROPE_THETA ** (-jnp.arange(half, dtype=F32) / half)
    ang = pos.astype(F32)[:, None] * inv[None, :]
    cos, sin = jnp.cos(ang), jnp.sin(ang)
    return (jnp.concatenate([cos, cos, cos, cos], axis=1),
            jnp.concatenate([-sin, sin, -sin, sin], axis=1))


def _tile_gain(g, reps):
    return jnp.tile(g.astype(F32), reps)[None, :]


def kernel(x_prompt, x_sample, cache_k, cache_v, state_conv, page_table, norm_mix, norm_ffn, w_in_even, w_out_even, q_norm, k_norm, lambda_qk, head_norm, w_dw, b_dw, conv_ln_g, conv_ln_b, w_uv, v_ln_g, v_ln_b, w_spatial, b_spatial, w_out_odd, w_router_group, b_router_group, w_router_expert, b_router_expert, w_expert_up, w_expert_down):
    batch, seq, _ = x_prompt.shape
    db = x_sample.shape[0]
    n_pages = page_table.shape[1]
    page = cache_k.shape[2]
    past_len = n_pages * page
    tm = TILE_TOKENS

    xp = x_prompt.reshape(batch * seq, D_MODEL)
    xs = x_sample.reshape(db, D_MODEL)
    cos_p, sin_p = _rope_tables(jnp.arange(seq))
    cos_s, sin_s = _rope_tables(jnp.full((db,), past_len))
    ck = cache_k.reshape(cache_k.shape[0], cache_k.shape[1], page * N_HEADS_A, HEAD_V)
    cv = cache_v.reshape(cache_v.shape[0], cache_v.shape[1], page * N_HEADS_A, HEAD_V)
    group_of = jnp.arange(D_A) // HEAD_DIM
    gmat32 = jnp.where(group_of[:, None] == group_of[None, :], 1.0 / HEAD_DIM, 0.0).astype(F32)
    gmat = gmat32.astype(BF16)
    w_in_bf = w_in_even.astype(BF16)
    wup_bf = w_expert_up.astype(BF16)
    wdn_bf = w_expert_down.astype(BF16)

    cp_l, cs_l, chs_l = [], [], []
    n_even = w_in_even.shape[0]
    kv_p = kv_s = None
    for layer in range(DEPTH):
        i = layer // 2
        g_mix = norm_mix[layer][None, :]
        hp = layer < HP_SAMPLE_LAYERS
        if layer % 2 == 0:
            lam_init = 0.8 - 0.6 * math.exp(-0.3 * layer)
            w_out = w_out_even[i].astype(BF16)
            qg = _tile_gain(q_norm[i], D_A // HEAD_DIM)
            kg = _tile_gain(k_norm[i], D_A // HEAD_DIM)
            og = head_norm[i][None, :]
            wdw = jnp.pad(w_dw[i], ((0, CONV_HALO - CONV_WIDTH), (0, 0)))
            bdw, lng, lnb = b_dw[i][None, :], conv_ln_g[i][None, :], conv_ln_b[i][None, :]
            q, kp_all, kb, vp_all, vb, hb = _even_in(xp, g_mix, w_in_bf, i, n_even, kv_p, qg, kg,
                                                     gmat, cos_p, sin_p, tm, seq // tm)
            kv_p = (kp_all, vp_all)
            o = _prompt_attention(q, kb, vb, lambda_qk[i], og, batch, seq, lam_init)
            xp = _even_out(hb, o, xp, wdw, bdw, lng, lnb, w_out, tm, seq)
            cp_l.append(hb.reshape(batch, seq, D_B)[:, seq - (CONV_WIDTH - 1):, :])
            qs, ks_all, _, vs_all, _, hbs = _even_in(
                xs, g_mix, w_in_even if hp else w_in_bf, i, n_even, kv_s, qg, kg,
                gmat32 if hp else gmat, cos_s, sin_s, db, 1, hp)
            kv_s = (ks_all, vs_all)
            os_ = _decode_attention(page_table, qs, ks_all[i], vs_all[i], lambda_qk[i], og, ck, cv,
                                    i, lam_init, hp)
            xs = _even_out_sample(state_conv[i], hbs, os_, xs, wdw, bdw, lng,
                                  lnb, w_out_even[i] if hp else w_out, hp)
            cs_l.append(jnp.concatenate([state_conv[i][:, 1:, :], hbs[:, None, :]], axis=1))
        else:
            wuv = w_uv[i].astype(BF16)
            wo = w_out_odd[i].astype(BF16)
            lng, lnb = v_ln_g[i][None, :], v_ln_b[i][None, :]
            gw = D_C // N_GROUPS_C
            bs = jnp.repeat(b_spatial[i].T, gw, axis=1)
            xp = _odd(xp, g_mix, wuv, lng, lnb, w_spatial[i], bs, wo, tm)
            ws0 = jnp.repeat(w_spatial[i][:, 0, 0], gw)[None, :]
            xs, vch = _odd_sample(xs, g_mix, w_uv[i] if hp else wuv, lng, lnb, ws0, bs[0:1, :],
                                  w_out_odd[i] if hp else wo, hp)
            chs_l.append(vch.reshape(db, 1, D_C))
        g_ffn = norm_ffn[layer][None, :]
        pad = ROUTER_LANES - N_EXPERT_GROUPS - N_EXPERTS
        wr32 = jnp.pad(jnp.concatenate([w_router_group[layer], w_router_expert[layer]], axis=1),
                       ((0, 0), (0, pad)))
        wr = wr32.astype(BF16)
        br = jnp.pad(jnp.concatenate([b_router_group[layer], b_router_expert[layer]]),
                     (0, pad))[None, :]
        xp = _moe(xp, g_ffn, wr, br, wup_bf, wdn_bf, layer, MOE_TILE_TOKENS)
        if hp:
            xs = _moe(xs, g_ffn, wr32, br, w_expert_up, w_expert_down, layer, db, True)
        else:
            xs = _moe(xs, g_ffn, wr, br, wup_bf, wdn_bf, layer, db)

    return (xp.reshape(batch, seq, D_MODEL), xs.reshape(db, 1, D_MODEL),
            kv_p[0].reshape(n_even, batch, seq, N_HEADS_A, HEAD_V),
            kv_p[1].reshape(n_even, batch, seq, N_HEADS_A, HEAD_V),
            kv_s[0].reshape(n_even, db, 1, N_HEADS_A, HEAD_V),
            kv_s[1].reshape(n_even, db, 1, N_HEADS_A, HEAD_V),
            jnp.stack(cp_l), jnp.stack(cs_l), jnp.stack(chs_l))
```

```python
import functools
import math

import jax
import jax.numpy as jnp
from jax import lax
from jax.experimental import pallas as pl
from jax.experimental.pallas import tpu as pltpu

F32 = jnp.float32
BF16 = jnp.bfloat16

D_MODEL = 1024
DEPTH = 4
N_HEADS_A = 4
HEAD_DIM = 64
HEAD_V = 2 * HEAD_DIM
D_A = N_HEADS_A * HEAD_V
D_B = D_MODEL // 2
CONV_WIDTH = 31
CONV_HALO = 32
D_IN_EVEN = 3 * D_A + 2 * D_B
D_C = D_MODEL
N_GROUPS_C = 4
CHUNK = 128
N_EXPERT_GROUPS = 4
EXPERTS_PER_GROUP = 4
N_EXPERTS = N_EXPERT_GROUPS * EXPERTS_PER_GROUP
D_FF_EXPERT = 256
ROPE_THETA = 10000.0
EPS = 1e-6
NEG_BIG = -1e30
ROUTER_LANES = 128
SUBLANES = 8

VMEM_LIMIT = 52 * 1024 * 1024

TILE_TOKENS = 512
EVEN_IN_SPLIT = 2
ODD_SPLIT = 1
MOE_TILE_TOKENS = 1024
ATTN_BLOCK = 512
ATTN_SUB = 256
ATTN_UNROLL = 4
CONV_ROWS = 64
PAGES_PER_STEP = 32
HP_SAMPLE_LAYERS = 2


def _params(sem):
    return pltpu.CompilerParams(dimension_semantics=sem, vmem_limit_bytes=VMEM_LIMIT)


def _rms(x, g):
    return x * lax.rsqrt(jnp.mean(x * x, axis=-1, keepdims=True) + EPS) * g


def _layer_norm(x, g, b):
    mu = jnp.mean(x, axis=-1, keepdims=True)
    xc = x - mu
    var = jnp.mean(xc * xc, axis=-1, keepdims=True)
    return xc * lax.rsqrt(var + EPS) * g + b


def _dot(a, b):
    return jnp.dot(a, b, preferred_element_type=F32)


def _dot_nt(a, b):
    return lax.dot_general(a, b, (((1,), (1,)), ((), ())), preferred_element_type=F32)


def _split(x):
    hi = x.astype(BF16)
    return hi, (x - hi.astype(F32)).astype(BF16)


def _stack_split(x):
    hi = x.astype(BF16).astype(F32)
    return jnp.concatenate([hi, x - hi], axis=0).astype(BF16)


def _mm(a, w, hp):
    if not hp:
        return _dot(a.astype(BF16), w)
    m = a.shape[0]
    wh, wl = _split(w)
    top = _dot(_stack_split(a), wh)
    return top[0:m] + top[m:2 * m] + _dot(a.astype(BF16), wl)


def _lambda(lp, lam_init):
    a = jnp.sum(lp[0:1] * lp[1:2], axis=1, keepdims=True)
    b = jnp.sum(lp[2:3] * lp[3:4], axis=1, keepdims=True)
    return jnp.exp(a) - jnp.exp(b) + lam_init


def _even_in_kernel(x_ref, g_ref, w_ref, qg_ref, kg_ref, gmat_ref, cos_ref, sin_ref, *rest,
                    hp, slot, first):
    q_ref, k32_ref, kb_ref, v32_ref, vb_ref, hb_ref = rest[-6:]
    tm = x_ref.shape[0]
    if first:
        for other in range(k32_ref.shape[0]):
            if other != slot:
                k32_ref[other] = jnp.zeros(k32_ref.shape[1:], F32)
                v32_ref[other] = jnp.zeros(v32_ref.shape[1:], F32)
        k_out, v_out = k32_ref.at[slot], v32_ref.at[slot]
    else:
        k_out, v_out = k32_ref, v32_ref
    lane = lax.broadcasted_iota(jnp.int32, (tm // EVEN_IN_SPLIT, D_A), 1)
    first_half = (lane % HEAD_DIM) < (HEAD_DIM // 2)

    for part in range(EVEN_IN_SPLIT):
        rows = tm // EVEN_IN_SPLIT
        r0 = part * rows
        sl = slice(r0, r0 + rows)
        h = _rms(x_ref[sl, :], g_ref[...])
        proj = _mm(h, w_ref[...], hp)
        cos = jnp.concatenate([cos_ref[sl, :]] * (D_A // 128), axis=1)
        sin = jnp.concatenate([sin_ref[sl, :]] * (D_A // 128), axis=1)

        def norm_rope(z, gain):
            ms = _mm(z * z, gmat_ref[...], hp)
            zn = z * lax.rsqrt(ms + EPS) * gain
            rot = jnp.where(first_half,
                            pltpu.roll(zn, D_A - HEAD_DIM // 2, 1),
                            pltpu.roll(zn, HEAD_DIM // 2, 1))
            return zn * cos + rot * sin

        q = norm_rope(proj[:, 0:D_A], qg_ref[...])
        k = norm_rope(proj[:, D_A:2 * D_A], kg_ref[...])
        v = proj[:, 2 * D_A:3 * D_A]
        a = proj[:, 3 * D_A:3 * D_A + D_B]
        g = proj[:, 3 * D_A + D_B:]
        q_ref[sl, :] = (q * (1.0 / math.sqrt(HEAD_DIM))).astype(q_ref.dtype)
        kb_ref[sl, :] = k.astype(BF16)
        vb_ref[sl, :] = v.astype(BF16)
        hb_ref[sl, :] = a * jax.nn.sigmoid(g)
        for hd in range(N_HEADS_A):
            cols = slice(hd * HEAD_V, (hd + 1) * HEAD_V)
            dst = pl.ds(r0 * N_HEADS_A + hd, rows, stride=N_HEADS_A)
            k_out[dst, :] = k[:, cols]
            v_out[dst, :] = v[:, cols]


def _even_in(x, g, w_in, slot, n_slots, kv_prev, qg, kg, gmat, cos, sin, tm, pos_blocks, hp=False):
    T = x.shape[0]
    row = lambda i: (i, 0)
    fix = lambda i: (0, 0)
    pos = lambda i: (i % pos_blocks, 0)
    stacked = jax.ShapeDtypeStruct((n_slots, T * N_HEADS_A, HEAD_V), F32)
    outs = [jax.ShapeDtypeStruct((T, D_A), F32 if hp else BF16),
            stacked,
            jax.ShapeDtypeStruct((T, D_A), BF16),
            stacked,
            jax.ShapeDtypeStruct((T, D_A), BF16),
            jax.ShapeDtypeStruct((T, D_B), F32)]
    wide = pl.BlockSpec((tm, D_A), row)
    first = kv_prev is None
    if first:
        tall = pl.BlockSpec((n_slots, tm * N_HEADS_A, HEAD_V), lambda i: (0, i, 0))
        extra_specs, extra_args, aliases = [], [], {}
    else:
        tall = pl.BlockSpec((None, tm * N_HEADS_A, HEAD_V), lambda i: (slot, i, 0))
        extra_specs = [pl.BlockSpec(memory_space=pl.ANY)] * 2
        extra_args = list(kv_prev)
        aliases = {8: 1, 9: 3}
    return pl.pallas_call(
        functools.partial(_even_in_kernel, hp=hp, slot=slot, first=first),
        grid=(T // tm,),
        in_specs=[pl.BlockSpec((tm, D_MODEL), row),
                  pl.BlockSpec((1, D_MODEL), fix),
                  pl.BlockSpec((None, D_MODEL, D_IN_EVEN), lambda i: (slot, 0, 0)),
                  pl.BlockSpec((1, D_A), fix),
                  pl.BlockSpec((1, D_A), fix),
                  pl.BlockSpec((D_A, D_A), fix),
                  pl.BlockSpec((tm, 128), pos),
                  pl.BlockSpec((tm, 128), pos)] + extra_specs,
        out_specs=[wide, tall, wide, tall, wide, pl.BlockSpec((tm, D_B), row)],
        out_shape=outs,
        input_output_aliases=aliases,
        compiler_params=_params(("parallel",)),
        name="even_in",
    )(x, g, w_in, qg, kg, gmat, cos, sin, *extra_args)


def _attn_kernel(q_ref, k_ref, v_ref, lam_ref, og_ref, o_ref, v1_ref, sa_ref, sb_ref, m_ref,
                 acc_ref, *, blk, sub, lam_init):
    qi = pl.program_id(2)
    per_blk = blk // sub
    assert per_blk % 2 == 0

    @pl.when(qi == 0)
    def _():
        v1_ref[:, 0:HEAD_V] = v_ref[...]
        v1_ref[:, HEAD_V:2 * HEAD_V] = jnp.ones((v1_ref.shape[0], HEAD_V), BF16)

    q = q_ref[...]
    lane = lax.broadcasted_iota(jnp.int32, q.shape, 1)
    zero = jnp.zeros_like(q)
    qq = jnp.concatenate([jnp.where(lane < HEAD_DIM, q, zero),
                          jnp.where(lane >= HEAD_DIM, q, zero)], axis=0)
    m_ref[...] = jnp.full(m_ref.shape, NEG_BIG, F32)
    acc_ref[...] = jnp.zeros(acc_ref.shape, F32)

    def scores(start):
        if not isinstance(start, int):
            start = pl.multiple_of(start, sub)
        return _dot_nt(qq, k_ref[pl.ds(start, sub), :])

    def consume(s, start, mask):
        vblk = v1_ref[pl.ds(start, sub), :]
        if mask is not None:
            s = jnp.where(mask, s, NEG_BIG)
        m_old = m_ref[...]
        m_new = jnp.maximum(m_old, jnp.max(s, axis=1, keepdims=True))
        alpha = jnp.exp(m_old - m_new)
        p = jnp.exp((s - jnp.concatenate([m_new] * (sub // 128), axis=1)).astype(BF16))
        acc_ref[...] = (jnp.concatenate([alpha, alpha], axis=1) * acc_ref[...]
                        + _dot(p, vblk))
        m_ref[...] = m_new

    bufs = (sa_ref, sb_ref)
    sa_ref[...] = scores(0)

    def full_blocks(first, count):
        for u in range(count * per_blk):
            start = pl.multiple_of(first * blk + u * sub, sub)
            bufs[(u + 1) % 2][...] = scores(start + sub)
            consume(bufs[u % 2][...], start, None)

    def body(t, carry):
        full_blocks(ATTN_UNROLL * t, ATTN_UNROLL)
        return carry

    trips = qi // ATTN_UNROLL
    lax.fori_loop(0, trips, body, 0)
    done = trips * ATTN_UNROLL
    size = ATTN_UNROLL // 2
    while size >= 1:
        take = ((qi - done) // size) % 2 == 1

        @pl.when(take)
        def _(done=done, size=size):
            full_blocks(done, size)

        done = done + jnp.where(take, size, 0)
        size //= 2

    r = lax.broadcasted_iota(jnp.int32, (2 * blk, sub), 0)
    c = lax.broadcasted_iota(jnp.int32, (2 * blk, sub), 1)
    r = jnp.where(r >= blk, r - blk, r)
    for u in range(per_blk):
        start = pl.multiple_of(qi * blk + u * sub, sub)
        if u + 1 < per_blk:
            bufs[(u + 1) % 2][...] = scores(start + sub)
        consume(bufs[u % 2][...], start, c + u * sub <= r)

    lam = _lambda(lam_ref[...], lam_init)
    acc = acc_ref[...]
    ratio = acc[:, 0:HEAD_V] / acc[:, HEAD_V:2 * HEAD_V]
    o = ratio[0:blk] - lam * ratio[blk:2 * blk]
    o_ref[...] = (_rms(o, og_ref[...]) * (1.0 - lam_init)).astype(o_ref.dtype)


def _prompt_attention(q, kb, vb, lam_p, og, batch, seq, lam_init):
    blk = ATTN_BLOCK
    nq = seq // blk
    sub = ATTN_SUB
    kernel = functools.partial(_attn_kernel, blk=blk, sub=sub, lam_init=lam_init)
    scr = [pltpu.VMEM((seq, 2 * HEAD_V), BF16), pltpu.VMEM((2 * blk, sub), F32),
           pltpu.VMEM((2 * blk, sub), F32), pltpu.VMEM((2 * blk, HEAD_V), F32),
           pltpu.VMEM((2 * blk, 2 * HEAD_V), F32)]
    return pl.pallas_call(
        kernel,
        grid=(batch, N_HEADS_A, nq),
        in_specs=[pl.BlockSpec((blk, HEAD_V), lambda b, h, i: (b * nq + i, h)),
                  pl.BlockSpec((seq, HEAD_V), lambda b, h, i: (b, h)),
                  pl.BlockSpec((seq, HEAD_V), lambda b, h, i: (b, h)),
                  pl.BlockSpec((4, HEAD_DIM), lambda b, h, i: (0, 0)),
                  pl.BlockSpec((1, HEAD_V), lambda b, h, i: (0, 0))],
        out_specs=pl.BlockSpec((blk, HEAD_V), lambda b, h, i: (b * nq + i, h)),
        out_shape=jax.ShapeDtypeStruct((batch * seq, D_A), BF16),
        scratch_shapes=scr,
        compiler_params=_params(("parallel", "parallel", "arbitrary")),
        name="prompt_attn",
    )(q, kb, vb, lam_p, og)


def _decode_kernel(pt_ref, q_ref, kn_ref, vn_ref, lam_ref, og_ref, *rest, pps, lam_init, hp):
    k_refs = rest[:pps]
    v_refs = rest[pps:2 * pps]
    o_ref = rest[2 * pps]
    m_ref, l_ref, acc_ref, r_ref = rest[2 * pps + 1:]
    s_id = pl.program_id(1)
    rows = 2 * N_HEADS_A
    prow = k_refs[0].shape[0]

    q8 = q_ref[...]
    qh = q8.astype(BF16)
    if hp:
        q_used = q8
        q_lhs = _stack_split(q8)
    else:
        q_used = qh.astype(F32)
    row = lax.broadcasted_iota(jnp.int32, (rows, prow), 0)
    col = lax.broadcasted_iota(jnp.int32, (rows, prow), 1)
    own = (col % N_HEADS_A) == (row // 2)

    @pl.when(s_id == 0)
    def _():
        m_ref[...] = jnp.full(m_ref.shape, NEG_BIG, F32)
        l_ref[...] = jnp.zeros(l_ref.shape, F32)
        acc_ref[...] = jnp.zeros(acc_ref.shape, F32)

    def scores(k):
        if not hp:
            s = _dot_nt(qh, k.astype(BF16))
        else:
            kh, kl = _split(k)
            top = _dot_nt(q_lhs, kh)
            s = top[0:rows] + top[rows:2 * rows] + _dot_nt(qh, kl)
        return jnp.where(own, s, NEG_BIG)

    s = jnp.concatenate([scores(k_refs[j][...]) for j in range(pps)], axis=1)
    m_old = m_ref[...]
    m_new = jnp.maximum(m_old, jnp.max(s, axis=1, keepdims=True))
    alpha = jnp.exp(m_old - m_new)
    p = jnp.exp(s - m_new)
    l_ref[...] = alpha * l_ref[...] + jnp.sum(p, axis=1, keepdims=True)
    ph = p.astype(BF16)
    if hp:
        p_lhs = _stack_split(p)
    pv = jnp.zeros((rows, HEAD_V), F32)
    for j in range(pps):
        cols = slice(j * prow, (j + 1) * prow)
        v = v_refs[j][...]
        if hp:
            vh, vl = _split(v)
            top = _dot(p_lhs[:, cols], vh)
            pv = pv + top[0:rows] + top[rows:2 * rows] + _dot(ph[:, cols], vl)
        else:
            pv = pv + _dot(ph[:, cols], v.astype(BF16))
    acc_ref[...] = alpha * acc_ref[...] + pv
    m_ref[...] = m_new

    @pl.when(s_id == pl.num_programs(1) - 1)
    def _():
        s_new = jnp.sum(q_used * kn_ref[...], axis=1, keepdims=True)
        m_old = m_ref[...]
        m_fin = jnp.maximum(m_old, s_new)
        alpha = jnp.exp(m_old - m_fin)
        p_new = jnp.exp(s_new - m_fin)
        l_fin = alpha * l_ref[...] + p_new
        r_ref[...] = (alpha * acc_ref[...] + p_new * vn_ref[...]) / l_fin
        lam = _lambda(lam_ref[...], lam_init)
        o = (r_ref[pl.ds(0, N_HEADS_A, stride=2), :]
             - lam * r_ref[pl.ds(1, N_HEADS_A, stride=2), :])
        o_ref[...] = _rms(o, og_ref[...]) * (1.0 - lam_init)


def _decode_attention(page_table, q, k_new, v_new, lam_p, og, cache_k, cache_v, layer_slot,
                      lam_init, hp):
    db, n_pages = page_table.shape
    pps = PAGES_PER_STEP
    prow = cache_k.shape[2]
    rows = 2 * N_HEADS_A
    half = jnp.arange(HEAD_V) // HEAD_DIM
    map_mask = (half[None, :] == jnp.arange(2)[:, None]).astype(F32)
    q8 = (q.astype(F32).reshape(db, N_HEADS_A, 1, HEAD_V) * map_mask).reshape(db, rows, HEAD_V)
    k8 = jnp.repeat(k_new.reshape(db, N_HEADS_A, HEAD_V), 2, axis=1)
    v8 = jnp.repeat(v_new.reshape(db, N_HEADS_A, HEAD_V), 2, axis=1)
    kernel = functools.partial(_decode_kernel, pps=pps, lam_init=lam_init, hp=hp)
    tok = pl.BlockSpec((None, rows, HEAD_V), lambda b, s, pt: (b, 0, 0))

    def page_spec(j):
        return pl.BlockSpec((None, None, prow, HEAD_V),
                            lambda b, s, pt: (layer_slot, pt[b * n_pages + s * pps + j], 0, 0))

    grid_spec = pltpu.PrefetchScalarGridSpec(
        num_scalar_prefetch=1,
        grid=(db, n_pages // pps),
        in_specs=[tok, tok, tok,
                  pl.BlockSpec((4, HEAD_DIM), lambda b, s, pt: (0, 0)),
                  pl.BlockSpec((1, HEAD_V), lambda b, s, pt: (0, 0))]
                 + [page_spec(j) for j in range(pps)] * 2,
        out_specs=pl.BlockSpec((None, N_HEADS_A, HEAD_V), lambda b, s, pt: (b, 0, 0)),
        scratch_shapes=[pltpu.VMEM((rows, 1), F32), pltpu.VMEM((rows, 1), F32),
                        pltpu.VMEM((rows, HEAD_V), F32), pltpu.VMEM((rows, HEAD_V), F32)],
    )
    out = pl.pallas_call(
        kernel,
        grid_spec=grid_spec,
        out_shape=jax.ShapeDtypeStruct((db, N_HEADS_A, HEAD_V), F32),
        compiler_params=_params(("parallel", "arbitrary")),
        name="decode_attn",
    )(page_table.reshape(-1), q8, k8, v8, lam_p, og, *([cache_k] * pps), *([cache_v] * pps))
    return out.reshape(db, D_A)


def _conv_tail(conv, lng, lnb, o, x, w_ref, hp=False):
    y = _layer_norm(conv, lng, lnb)
    c = y * jax.nn.sigmoid(y)
    return (x + _mm(o, w_ref[0:D_A, :], hp) + _mm(c, w_ref[D_A:D_A + D_B, :], hp))


def _even_out_kernel(hb_ref, hprev_ref, o_ref, x_ref, wdw_ref, bdw_ref, lng_ref, lnb_ref,
                     w_ref, y_ref, win_ref, conv_ref, *, tm, tiles_per_seq):
    i = pl.program_id(0)
    first = (i % tiles_per_seq) == 0
    win_ref[0, 0:CONV_HALO, :] = jnp.where(first, 0.0, hprev_ref[...])
    win_ref[0, CONV_HALO:CONV_HALO + tm, :] = hb_ref[...]
    lead = CONV_HALO - (CONV_WIDTH - 1)
    shifted_rows = tm + CONV_HALO - SUBLANES
    for s in range(1, SUBLANES):
        win_ref[s, 0:shifted_rows, :] = win_ref[0, s:s + shifted_rows, :]
    bias = bdw_ref[...]
    for r0 in range(0, tm, CONV_ROWS):
        acc = jnp.broadcast_to(bias, (CONV_ROWS, D_B))
        for j in range(CONV_WIDTH):
            s = (lead + j) % SUBLANES
            a = r0 + lead + j - s
            acc = acc + win_ref[s, a:a + CONV_ROWS, :] * wdw_ref[j:j + 1, :]
        conv_ref[r0:r0 + CONV_ROWS, :] = acc
    y_ref[...] = _conv_tail(conv_ref[...], lng_ref[...], lnb_ref[...], o_ref[...], x_ref[...],
                            w_ref)


def _even_out(hb, o, x, wdw, bdw, lng, lnb, w_out, tm, seq):
    T = x.shape[0]
    tiles_per_seq = seq // tm
    halo_blocks = tm // CONV_HALO
    row = lambda i: (i, 0)
    fix = lambda i: (0, 0)
    kernel = functools.partial(_even_out_kernel, tm=tm, tiles_per_seq=tiles_per_seq)
    return pl.pallas_call(
        kernel,
        grid=(T // tm,),
        in_specs=[pl.BlockSpec((tm, D_B), row),
                  pl.BlockSpec((CONV_HALO, D_B), lambda i: (jnp.maximum(i * halo_blocks - 1, 0), 0)),
                  pl.BlockSpec((tm, D_A), row),
                  pl.BlockSpec((tm, D_MODEL), row),
                  pl.BlockSpec((CONV_HALO, D_B), fix),
                  pl.BlockSpec((1, D_B), fix),
                  pl.BlockSpec((1, D_B), fix),
                  pl.BlockSpec((1, D_B), fix),
                  pl.BlockSpec((D_A + D_B, D_MODEL), fix)],
        out_specs=pl.BlockSpec((tm, D_MODEL), row),
        out_shape=jax.ShapeDtypeStruct((T, D_MODEL), F32),
        scratch_shapes=[pltpu.VMEM((SUBLANES, CONV_HALO + tm, D_B), F32),
                        pltpu.VMEM((tm, D_B), F32)],
        compiler_params=_params(("parallel",)),
        name="even_out",
    )(hb, hb, o, x, wdw, bdw, lng, lnb, w_out)


def _even_out_sample_kernel(st_ref, hb_ref, o_ref, x_ref, wdw_ref, bdw_ref, lng_ref, lnb_ref,
                            w_ref, y_ref, *, hp):
    w = wdw_ref[...]
    conv = jnp.sum(st_ref[...] * w[0:CONV_WIDTH - 1, :][None], axis=1)
    conv = conv + hb_ref[...] * w[CONV_WIDTH - 1:CONV_WIDTH, :] + bdw_ref[...]
    y_ref[...] = _conv_tail(conv, lng_ref[...], lnb_ref[...], o_ref[...], x_ref[...], w_ref,
                            hp)


def _even_out_sample(state, hb, o, x, wdw, bdw, lng, lnb, w_out, hp):
    db = x.shape[0]
    return pl.pallas_call(
        functools.partial(_even_out_sample_kernel, hp=hp),
        out_shape=jax.ShapeDtypeStruct((db, D_MODEL), F32),
        compiler_params=pltpu.CompilerParams(vmem_limit_bytes=VMEM_LIMIT),
        name="even_out_sample",
    )(state, hb, o, x, wdw, bdw, lng, lnb, w_out)


def _odd_front(x, g_ref, wuv_ref, lng_ref, lnb_ref, hp=False):
    h = _rms(x, g_ref[...])
    z = _mm(h, wuv_ref[...], hp)
    z = 0.5 * z * (1.0 + lax.erf(z * (1.0 / math.sqrt(2.0))))
    u = z[:, 0:D_C]
    v = _layer_norm(z[:, D_C:2 * D_C], lng_ref[...], lnb_ref[...])
    return u, v


def _odd_kernel(x_ref, g_ref, wuv_ref, lng_ref, lnb_ref, ws_ref, bs_ref, wo_ref, y_ref,
                sv_ref, *, tm):
    r = lax.broadcasted_iota(jnp.int32, (CHUNK, CHUNK), 0)
    c = lax.broadcasted_iota(jnp.int32, (CHUNK, CHUNK), 1)
    gw = D_C // N_GROUPS_C
    wsg = [jnp.where(c <= r, ws_ref[g], 0.0).astype(BF16)
           for g in range(N_GROUPS_C)]
    slab = tm // ODD_SPLIT
    for part in range(ODD_SPLIT):
        sl = slice(part * slab, (part + 1) * slab)
        x = x_ref[sl, :]
        u, v = _odd_front(x, g_ref, wuv_ref, lng_ref, lnb_ref)
        vb = v.astype(BF16)
        for g in range(N_GROUPS_C):
            cols = slice(g * gw, (g + 1) * gw)
            for ch in range(slab // CHUNK):
                rows = slice(ch * CHUNK, (ch + 1) * CHUNK)
                dst = slice(part * slab + ch * CHUNK, part * slab + (ch + 1) * CHUNK)
                sv_ref[dst, cols] = _dot(wsg[g], vb[rows, cols]) + bs_ref[:, cols]
        y_ref[sl, :] = x + _dot((u * sv_ref[sl, :]).astype(BF16), wo_ref[...])


def _odd(x, g, wuv, lng, lnb, ws, bs, wo, tm):
    T = x.shape[0]
    row = lambda i: (i, 0)
    fix = lambda i: (0, 0)
    return pl.pallas_call(
        functools.partial(_odd_kernel, tm=tm),
        grid=(T // tm,),
        in_specs=[pl.BlockSpec((tm, D_MODEL), row),
                  pl.BlockSpec((1, D_MODEL), fix),
                  pl.BlockSpec((D_MODEL, 2 * D_C), fix),
                  pl.BlockSpec((1, D_C), fix),
                  pl.BlockSpec((1, D_C), fix),
                  pl.BlockSpec((N_GROUPS_C, CHUNK, CHUNK), lambda i: (0, 0, 0)),
                  pl.BlockSpec((CHUNK, D_C), fix),
                  pl.BlockSpec((D_C, D_MODEL), fix)],
        out_specs=pl.BlockSpec((tm, D_MODEL), row),
        out_shape=jax.ShapeDtypeStruct((T, D_MODEL), F32),
        scratch_shapes=[pltpu.VMEM((tm, D_C), F32)],
        compiler_params=_params(("parallel",)),
        name="odd_mixer",
    )(x, g, wuv, lng, lnb, ws, bs, wo)


def _odd_sample_kernel(x_ref, g_ref, wuv_ref, lng_ref, lnb_ref, ws0_ref, bs0_ref, wo_ref,
                       y_ref, v_ref, *, hp):
    u, v = _odd_front(x_ref[...], g_ref, wuv_ref, lng_ref, lnb_ref, hp)
    v_ref[...] = v
    sv = v * ws0_ref[...] + bs0_ref[...]
    y_ref[...] = x_ref[...] + _mm(u * sv, wo_ref[...], hp)


def _odd_sample(x, g, wuv, lng, lnb, ws0, bs0, wo, hp):
    db = x.shape[0]
    return pl.pallas_call(
        functools.partial(_odd_sample_kernel, hp=hp),
        out_shape=[jax.ShapeDtypeStruct((db, D_MODEL), F32),
                   jax.ShapeDtypeStruct((db, D_C), F32)],
        compiler_params=pltpu.CompilerParams(vmem_limit_bytes=VMEM_LIMIT),
        name="odd_mixer_sample",
    )(x, g, wuv, lng, lnb, ws0, bs0, wo)


def _route(logits):
    lane = lax.broadcasted_iota(jnp.int32, logits.shape, 1)
    lane_f = lane.astype(F32)
    none = float(ROUTER_LANES)
    is_g = lane < N_EXPERT_GROUPS
    gl = jnp.where(is_g, logits, NEG_BIG)
    gmax = jnp.max(gl, axis=1, keepdims=True)
    gidx = jnp.min(jnp.where(is_g & (gl == gmax), lane_f, none), axis=1, keepdims=True)
    gw = 1.0 / jnp.sum(jnp.where(is_g, jnp.exp(gl - gmax), 0.0), axis=1, keepdims=True)
    e_f = lane_f - float(N_EXPERT_GROUPS)
    is_e = (lane >= N_EXPERT_GROUPS) & (lane < N_EXPERT_GROUPS + N_EXPERTS)
    grp_lo = gidx * float(EXPERTS_PER_GROUP)
    in_grp = is_e & (e_f >= grp_lo) & (e_f < grp_lo + float(EXPERTS_PER_GROUP))
    cand = jnp.where(in_grp, logits, NEG_BIG)
    v1 = jnp.max(cand, axis=1, keepdims=True)
    i1 = jnp.min(jnp.where(in_grp & (cand == v1), lane_f, none), axis=1, keepdims=True)
    rest = in_grp & (lane_f != i1)
    cand2 = jnp.where(rest, logits, NEG_BIG)
    v2 = jnp.max(cand2, axis=1, keepdims=True)
    i2 = jnp.min(jnp.where(rest & (cand2 == v2), lane_f, none), axis=1, keepdims=True)
    t = jnp.exp(v2 - v1)
    p1 = 1.0 / (1.0 + t)
    p2 = t * p1
    return jnp.where(lane_f == i1, p1 * gw, jnp.where(lane_f == i2, p2 * gw, 0.0))


def _moe_kernel(x_ref, g_ref, wr_ref, br_ref, wup_ref, wdn_ref, y_ref, xn_ref, gate_ref,
                acc_ref, *, hp):
    grp = pl.program_id(1)

    @pl.when(grp == 0)
    def _():
        xn = _rms(x_ref[...], g_ref[...]).astype(xn_ref.dtype)
        xn_ref[...] = xn
        gate_ref[...] = _route(_mm(xn, wr_ref[...], hp) + br_ref[...])
        acc_ref[...] = jnp.zeros(acc_ref.shape, F32)

    xn = xn_ref[...]
    gate = gate_ref[...]
    lane = lax.broadcasted_iota(jnp.int32, gate.shape, 1)
    acc = acc_ref[...]
    for j in range(EXPERTS_PER_GROUP):
        e_lane = N_EXPERT_GROUPS + grp * EXPERTS_PER_GROUP + j
        ge = jnp.sum(jnp.where(lane == e_lane, gate, 0.0), axis=1, keepdims=True)
        hu = _mm(xn, wup_ref[j], hp)
        a = hu[:, 0:D_FF_EXPERT]
        b = hu[:, D_FF_EXPERT:]
        act = a * jax.nn.sigmoid(a) * b * ge
        acc = acc + _mm(act, wdn_ref[j], hp)
    acc_ref[...] = acc

    @pl.when(grp == pl.num_programs(1) - 1)
    def _():
        y_ref[...] = x_ref[...] + acc_ref[...]


def _moe(x, g, wr, br, wup, wdn, layer, tm, hp=False):
    T = x.shape[0]
    row = lambda i, e: (i, 0)
    fix = lambda i, e: (0, 0)
    return pl.pallas_call(
        functools.partial(_moe_kernel, hp=hp),
        grid=(T // tm, N_EXPERT_GROUPS),
        in_specs=[pl.BlockSpec((tm, D_MODEL), row),
                  pl.BlockSpec((1, D_MODEL), fix),
                  pl.BlockSpec((D_MODEL, ROUTER_LANES), fix),
                  pl.BlockSpec((1, ROUTER_LANES), fix),
                  pl.BlockSpec((None, EXPERTS_PER_GROUP, D_MODEL, 2 * D_FF_EXPERT),
                               lambda i, e: (layer, e, 0, 0)),
                  pl.BlockSpec((None, EXPERTS_PER_GROUP, D_FF_EXPERT, D_MODEL),
                               lambda i, e: (layer, e, 0, 0))],
        out_specs=pl.BlockSpec((tm, D_MODEL), row),
        out_shape=jax.ShapeDtypeStruct((T, D_MODEL), F32),
        scratch_shapes=[pltpu.VMEM((tm, D_MODEL), F32 if hp else BF16),
                        pltpu.VMEM((tm, ROUTER_LANES), F32),
                        pltpu.VMEM((tm, D_MODEL), F32)],
        compiler_params=_params(("parallel", "arbitrary")),
        name="hier_moe",
    )(x, g, wr, br, wup, wdn)


def _rope_tables(pos):
    half = HEAD_DIM // 2
    inv = ROPE_THETA ** (-jnp.arange(half, dtype=F32) / half)
    ang = pos.astype(F32)[:, None] * inv[None, :]
    cos, sin = jnp.cos(ang), jnp.sin(ang)
    return (jnp.concatenate([cos, cos, cos, cos], axis=1),
            jnp.concatenate([-sin, sin, -sin, sin], axis=1))


def _tile_gain(g, reps):
    return jnp.tile(g.astype(F32), reps)[None, :]


def kernel(x_prompt, x_sample, cache_k, cache_v, state_conv, page_table, norm_mix, norm_ffn, w_in_even, w_out_even, q_norm, k_norm, lambda_qk, head_norm, w_dw, b_dw, conv_ln_g, conv_ln_b, w_uv, v_ln_g, v_ln_b, w_spatial, b_spatial, w_out_odd, w_router_group, b_router_group, w_router_expert, b_router_expert, w_expert_up, w_expert_down):
    batch, seq, _ = x_prompt.shape
    db = x_sample.shape[0]
    n_pages = page_table.shape[1]
    page = cache_k.shape[2]
    past_len = n_pages * page
    tm = TILE_TOKENS

    xp = x_prompt.reshape(batch * seq, D_MODEL)
    xs = x_sample.reshape(db, D_MODEL)
    cos_p, sin_p = _rope_tables(jnp.arange(seq))
    cos_s, sin_s = _rope_tables(jnp.full((db,), past_len))
    ck = cache_k.reshape(cache_k.shape[0], cache_k.shape[1], page * N_HEADS_A, HEAD_V)
    cv = cache_v.reshape(cache_v.shape[0], cache_v.shape[1], page * N_HEADS_A, HEAD_V)
    group_of = jnp.arange(D_A) // HEAD_DIM
    gmat32 = jnp.where(group_of[:, None] == group_of[None, :], 1.0 / HEAD_DIM, 0.0).astype(F32)
    gmat = gmat32.astype(BF16)
    w_in_bf = w_in_even.astype(BF16)
    wup_bf = w_expert_up.astype(BF16)
    wdn_bf = w_expert_down.astype(BF16)

    cp_l, cs_l, chs_l = [], [], []
    n_even = w_in_even.shape[0]
    kv_p = kv_s = None
    for layer in range(DEPTH):
        i = layer // 2
        g_mix = norm_mix[layer][None, :]
        hp = layer < HP_SAMPLE_LAYERS
        if layer % 2 == 0:
            lam_init = 0.8 - 0.6 * math.exp(-0.3 * layer)
            w_out = w_out_even[i].astype(BF16)
            qg = _tile_gain(q_norm[i], D_A // HEAD_DIM)
            kg = _tile_gain(k_norm[i], D_A // HEAD_DIM)
            og = head_norm[i][None, :]
            wdw = jnp.pad(w_dw[i], ((0, CONV_HALO - CONV_WIDTH), (0, 0)))
            bdw, lng, lnb = b_dw[i][None, :], conv_ln_g[i][None, :], conv_ln_b[i][None, :]
            q, kp_all, kb, vp_all, vb, hb = _even_in(xp, g_mix, w_in_bf, i, n_even, kv_p, qg, kg,
                                                     gmat, cos_p, sin_p, tm, seq // tm)
            kv_p = (kp_all, vp_all)
            o = _prompt_attention(q, kb, vb, lambda_qk[i], og, batch, seq, lam_init)
            xp = _even_out(hb, o, xp, wdw, bdw, lng, lnb, w_out, tm, seq)
            cp_l.append(hb.reshape(batch, seq, D_B)[:, seq - (CONV_WIDTH - 1):, :])
            qs, ks_all, _, vs_all, _, hbs = _even_in(
                xs, g_mix, w_in_even if hp else w_in_bf, i, n_even, kv_s, qg, kg,
                gmat32 if hp else gmat, cos_s, sin_s, db, 1, hp)
            kv_s = (ks_all, vs_all)
            os_ = _decode_attention(page_table, qs, ks_all[i], vs_all[i], lambda_qk[i], og, ck, cv,
                                    i, lam_init, hp)
            xs = _even_out_sample(state_conv[i], hbs, os_, xs, wdw, bdw, lng,
                                  lnb, w_out_even[i] if hp else w_out, hp)
            cs_l.append(jnp.concatenate([state_conv[i][:, 1:, :], hbs[:, None, :]], axis=1))
        else:
            wuv = w_uv[i].astype(BF16)
            wo = w_out_odd[i].astype(BF16)
            lng, lnb = v_ln_g[i][None, :], v_ln_b[i][None, :]
            gw = D_C // N_GROUPS_C
            bs = jnp.repeat(b_spatial[i].T, gw, axis=1)
            xp = _odd(xp, g_mix, wuv, lng, lnb, w_spatial[i], bs, wo, tm)
            ws0 = jnp.repeat(w_spatial[i][:, 0, 0], gw)[None, :]
            xs, vch = _odd_sample(xs, g_mix, w_uv[i] if hp else wuv, lng, lnb, ws0, bs[0:1, :],
                                  w_out_odd[i] if hp else wo, hp)
            chs_l.append(vch.reshape(db, 1, D_C))
        g_ffn = norm_ffn[layer][None, :]
        pad = ROUTER_LANES - N_EXPERT_GROUPS - N_EXPERTS
        wr32 = jnp.pad(jnp.concatenate([w_router_group[layer], w_router_expert[layer]], axis=1),
                       ((0, 0), (0, pad)))
        wr = wr32.astype(BF16)
        br = jnp.pad(jnp.concatenate([b_router_group[layer], b_router_expert[layer]]),
                     (0, pad))[None, :]
        xp = _moe(xp, g_ffn, wr, br, wup_bf, wdn_bf, layer, MOE_TILE_TOKENS)
        if hp:
            xs = _moe(xs, g_ffn, wr32, br, w_expert_up, w_expert_down, layer, db, True)
        else:
            xs = _moe(xs, g_ffn, wr, br, wup_bf, wdn_bf, layer, db)

    return (xp.reshape(batch, seq, D_MODEL), xs.reshape(db, 1, D_MODEL),
            kv_p[0].reshape(n_even, batch, seq, N_HEADS_A, HEAD_V),
            kv_p[1].reshape(n_even, batch, seq, N_HEADS_A, HEAD_V),
            kv_s[0].reshape(n_even, db, 1, N_HEADS_A, HEAD_V),
            kv_s[1].reshape(n_even, db, 1, N_HEADS_A, HEAD_V),
            jnp.stack(cp_l), jnp.stack(cs_l), jnp.stack(chs_l))
```

```python
import functools
import math

import jax
import jax.numpy as jnp
from jax import lax
from jax.experimental import pallas as pl
from jax.experimental.pallas import tpu as pltpu

F32 = jnp.float32
BF16 = jnp.bfloat16

D_MODEL = 1024
DEPTH = 4
N_HEADS_A = 4
HEAD_DIM = 64
HEAD_V = 2 * HEAD_DIM
D_A = N_HEADS_A * HEAD_V
D_B = D_MODEL // 2
CONV_WIDTH = 31
CONV_HALO = 32
D_IN_EVEN = 3 * D_A + 2 * D_B
D_C = D_MODEL
N_GROUPS_C = 4
CHUNK = 128
N_EXPERT_GROUPS = 4
EXPERTS_PER_GROUP = 4
N_EXPERTS = N_EXPERT_GROUPS * EXPERTS_PER_GROUP
D_FF_EXPERT = 256
ROPE_THETA = 10000.0
EPS = 1e-6
NEG_BIG = -1e30
ROUTER_LANES = 128
SUBLANES = 8

VMEM_LIMIT = 52 * 1024 * 1024
MOE_VMEM_LIMIT = 58 * 1024 * 1024

TILE_TOKENS = 512
EVEN_IN_SPLIT = 2
ODD_SPLIT = 1
MOE_TILE_TOKENS = 1024
ATTN_BLOCK = 512
ATTN_SUB = 256
ATTN_UNROLL = 4
CONV_ROWS = 64
PAGES_PER_STEP = 32
HP_SAMPLE_LAYERS = 2


def _params(sem, limit=None):
    return pltpu.CompilerParams(dimension_semantics=sem, vmem_limit_bytes=limit or VMEM_LIMIT)


def _rms(x, g):
    return x * lax.rsqrt(jnp.mean(x * x, axis=-1, keepdims=True) + EPS) * g


def _layer_norm(x, g, b):
    mu = jnp.mean(x, axis=-1, keepdims=True)
    xc = x - mu
    var = jnp.mean(xc * xc, axis=-1, keepdims=True)
    return xc * lax.rsqrt(var + EPS) * g + b


def _dot(a, b):
    return jnp.dot(a, b, preferred_element_type=F32)


def _dot_nt(a, b):
    return lax.dot_general(a, b, (((1,), (1,)), ((), ())), preferred_element_type=F32)


def _split(x):
    hi = x.astype(BF16)
    return hi, (x - hi.astype(F32)).astype(BF16)


def _stack_split(x):
    hi = x.astype(BF16).astype(F32)
    return jnp.concatenate([hi, x - hi], axis=0).astype(BF16)


def _mm(a, w, hp):
    if not hp:
        return _dot(a.astype(BF16), w.astype(BF16))
    m = a.shape[0]
    wh, wl = _split(w)
    top = _dot(_stack_split(a), wh)
    return top[0:m] + top[m:2 * m] + _dot(a.astype(BF16), wl)


def _lambda(lp, lam_init):
    a = jnp.sum(lp[0:1] * lp[1:2], axis=1, keepdims=True)
    b = jnp.sum(lp[2:3] * lp[3:4], axis=1, keepdims=True)
    return jnp.exp(a) - jnp.exp(b) + lam_init


def _even_in_kernel(x_ref, g_ref, w_ref, qg_ref, kg_ref, gmat_ref, cos_ref, sin_ref, *rest,
                    hp, slot, first):
    q_ref, k32_ref, kb_ref, v32_ref, vb_ref, hb_ref = rest[-6:]
    tm = x_ref.shape[0]
    if first:
        for other in range(k32_ref.shape[0]):
            if other != slot:
                k32_ref[other] = jnp.zeros(k32_ref.shape[1:], F32)
                v32_ref[other] = jnp.zeros(v32_ref.shape[1:], F32)
        k_out, v_out = k32_ref.at[slot], v32_ref.at[slot]
    else:
        k_out, v_out = k32_ref, v32_ref
    lane = lax.broadcasted_iota(jnp.int32, (tm // EVEN_IN_SPLIT, D_A), 1)
    first_half = (lane % HEAD_DIM) < (HEAD_DIM // 2)

    for part in range(EVEN_IN_SPLIT):
        rows = tm // EVEN_IN_SPLIT
        r0 = part * rows
        sl = slice(r0, r0 + rows)
        h = _rms(x_ref[sl, :], g_ref[...])
        proj = _mm(h, w_ref[...], hp)
        cos = jnp.concatenate([cos_ref[sl, :]] * (D_A // 128), axis=1)
        sin = jnp.concatenate([sin_ref[sl, :]] * (D_A // 128), axis=1)

        def norm_rope(z, gain):
            ms = _mm(z * z, gmat_ref[...], hp)
            zn = z * lax.rsqrt(ms + EPS) * gain
            rot = jnp.where(first_half,
                            pltpu.roll(zn, D_A - HEAD_DIM // 2, 1),
                            pltpu.roll(zn, HEAD_DIM // 2, 1))
            return zn * cos + rot * sin

        q = norm_rope(proj[:, 0:D_A], qg_ref[...])
        k = norm_rope(proj[:, D_A:2 * D_A], kg_ref[...])
        v = proj[:, 2 * D_A:3 * D_A]
        a = proj[:, 3 * D_A:3 * D_A + D_B]
        g = proj[:, 3 * D_A + D_B:]
        q_ref[sl, :] = (q * (1.0 / math.sqrt(HEAD_DIM))).astype(q_ref.dtype)
        kb_ref[sl, :] = k.astype(BF16)
        vb_ref[sl, :] = v.astype(BF16)
        hb_ref[sl, :] = a * jax.nn.sigmoid(g)
        for hd in range(N_HEADS_A):
            cols = slice(hd * HEAD_V, (hd + 1) * HEAD_V)
            dst = pl.ds(r0 * N_HEADS_A + hd, rows, stride=N_HEADS_A)
            k_out[dst, :] = k[:, cols]
            v_out[dst, :] = v[:, cols]


def _even_in(x, g, w_in, slot, n_slots, kv_prev, qg, kg, gmat, cos, sin, tm, pos_blocks, hp=False):
    T = x.shape[0]
    row = lambda i: (i, 0)
    fix = lambda i: (0, 0)
    pos = lambda i: (i % pos_blocks, 0)
    stacked = jax.ShapeDtypeStruct((n_slots, T * N_HEADS_A, HEAD_V), F32)
    outs = [jax.ShapeDtypeStruct((T, D_A), F32 if hp else BF16),
            stacked,
            jax.ShapeDtypeStruct((T, D_A), BF16),
            stacked,
            jax.ShapeDtypeStruct((T, D_A), BF16),
            jax.ShapeDtypeStruct((T, D_B), F32)]
    wide = pl.BlockSpec((tm, D_A), row)
    first = kv_prev is None
    if first:
        tall = pl.BlockSpec((n_slots, tm * N_HEADS_A, HEAD_V), lambda i: (0, i, 0))
        extra_specs, extra_args, aliases = [], [], {}
    else:
        tall = pl.BlockSpec((None, tm * N_HEADS_A, HEAD_V), lambda i: (slot, i, 0))
        extra_specs = [pl.BlockSpec(memory_space=pl.ANY)] * 2
        extra_args = list(kv_prev)
        aliases = {8: 1, 9: 3}
    return pl.pallas_call(
        functools.partial(_even_in_kernel, hp=hp, slot=slot, first=first),
        grid=(T // tm,),
        in_specs=[pl.BlockSpec((tm, D_MODEL), row),
                  pl.BlockSpec((1, D_MODEL), fix),
                  pl.BlockSpec((None, D_MODEL, D_IN_EVEN), lambda i: (slot, 0, 0)),
                  pl.BlockSpec((1, D_A), fix),
                  pl.BlockSpec((1, D_A), fix),
                  pl.BlockSpec((D_A, D_A), fix),
                  pl.BlockSpec((tm, 128), pos),
                  pl.BlockSpec((tm, 128), pos)] + extra_specs,
        out_specs=[wide, tall, wide, tall, wide, pl.BlockSpec((tm, D_B), row)],
        out_shape=outs,
        input_output_aliases=aliases,
        compiler_params=_params(("parallel",)),
        name="even_in",
    )(x, g, w_in, qg, kg, gmat, cos, sin, *extra_args)


def _attn_kernel(q_ref, k_ref, v_ref, lam_ref, og_ref, o_ref, v1_ref, sa_ref, sb_ref, m_ref,
                 acc_ref, *, blk, sub, lam_init):
    qi = pl.program_id(2)
    per_blk = blk // sub
    assert per_blk % 2 == 0

    @pl.when(qi == 0)
    def _():
        v1_ref[:, 0:HEAD_V] = v_ref[...]
        v1_ref[:, HEAD_V:2 * HEAD_V] = jnp.ones((v1_ref.shape[0], HEAD_V), BF16)

    q = q_ref[...]
    lane = lax.broadcasted_iota(jnp.int32, q.shape, 1)
    zero = jnp.zeros_like(q)
    qq = jnp.concatenate([jnp.where(lane < HEAD_DIM, q, zero),
                          jnp.where(lane >= HEAD_DIM, q, zero)], axis=0)
    m_ref[...] = jnp.full(m_ref.shape, NEG_BIG, F32)
    acc_ref[...] = jnp.zeros(acc_ref.shape, F32)

    def scores(start):
        if not isinstance(start, int):
            start = pl.multiple_of(start, sub)
        return _dot_nt(qq, k_ref[pl.ds(start, sub), :])

    def consume(s, start, mask):
        vblk = v1_ref[pl.ds(start, sub), :]
        if mask is not None:
            s = jnp.where(mask, s, NEG_BIG)
        m_old = m_ref[...]
        m_new = jnp.maximum(m_old, jnp.max(s, axis=1, keepdims=True))
        alpha = jnp.exp(m_old - m_new)
        p = jnp.exp((s - jnp.concatenate([m_new] * (sub // 128), axis=1)).astype(BF16))
        acc_ref[...] = (jnp.concatenate([alpha, alpha], axis=1) * acc_ref[...]
                        + _dot(p, vblk))
        m_ref[...] = m_new

    bufs = (sa_ref, sb_ref)
    sa_ref[...] = scores(0)

    def full_blocks(first, count):
        for u in range(count * per_blk):
            start = pl.multiple_of(first * blk + u * sub, sub)
            bufs[(u + 1) % 2][...] = scores(start + sub)
            consume(bufs[u % 2][...], start, None)

    def body(t, carry):
        full_blocks(ATTN_UNROLL * t, ATTN_UNROLL)
        return carry

    trips = qi // ATTN_UNROLL
    lax.fori_loop(0, trips, body, 0)
    done = trips * ATTN_UNROLL
    size = ATTN_UNROLL // 2
    while size >= 1:
        take = ((qi - done) // size) % 2 == 1

        @pl.when(take)
        def _(done=done, size=size):
            full_blocks(done, size)

        done = done + jnp.where(take, size, 0)
        size //= 2

    r = lax.broadcasted_iota(jnp.int32, (2 * blk, sub), 0)
    c = lax.broadcasted_iota(jnp.int32, (2 * blk, sub), 1)
    r = jnp.where(r >= blk, r - blk, r)
    for u in range(per_blk):
        start = pl.multiple_of(qi * blk + u * sub, sub)
        if u + 1 < per_blk:
            bufs[(u + 1) % 2][...] = scores(start + sub)
        consume(bufs[u % 2][...], start, c + u * sub <= r)

    lam = _lambda(lam_ref[...], lam_init)
    acc = acc_ref[...]
    ratio = acc[:, 0:HEAD_V] / acc[:, HEAD_V:2 * HEAD_V]
    o = ratio[0:blk] - lam * ratio[blk:2 * blk]
    o_ref[...] = (_rms(o, og_ref[...]) * (1.0 - lam_init)).astype(o_ref.dtype)


def _prompt_attention(q, kb, vb, lam_p, og, batch, seq, lam_init):
    blk = ATTN_BLOCK
    nq = seq // blk
    sub = ATTN_SUB
    kernel = functools.partial(_attn_kernel, blk=blk, sub=sub, lam_init=lam_init)
    scr = [pltpu.VMEM((seq, 2 * HEAD_V), BF16), pltpu.VMEM((2 * blk, sub), F32),
           pltpu.VMEM((2 * blk, sub), F32), pltpu.VMEM((2 * blk, HEAD_V), F32),
           pltpu.VMEM((2 * blk, 2 * HEAD_V), F32)]
    return pl.pallas_call(
        kernel,
        grid=(batch, N_HEADS_A, nq),
        in_specs=[pl.BlockSpec((blk, HEAD_V), lambda b, h, i: (b * nq + i, h)),
                  pl.BlockSpec((seq, HEAD_V), lambda b, h, i: (b, h)),
                  pl.BlockSpec((seq, HEAD_V), lambda b, h, i: (b, h)),
                  pl.BlockSpec((4, HEAD_DIM), lambda b, h, i: (0, 0)),
                  pl.BlockSpec((1, HEAD_V), lambda b, h, i: (0, 0))],
        out_specs=pl.BlockSpec((blk, HEAD_V), lambda b, h, i: (b * nq + i, h)),
        out_shape=jax.ShapeDtypeStruct((batch * seq, D_A), BF16),
        scratch_shapes=scr,
        compiler_params=_params(("parallel", "parallel", "arbitrary")),
        name="prompt_attn",
    )(q, kb, vb, lam_p, og)


def _decode_kernel(pt_ref, q_ref, kn_ref, vn_ref, lam_ref, og_ref, *rest, pps, lam_init, hp):
    k_refs = rest[:pps]
    v_refs = rest[pps:2 * pps]
    o_ref = rest[2 * pps]
    m_ref, l_ref, acc_ref, r_ref = rest[2 * pps + 1:]
    s_id = pl.program_id(1)
    rows = 2 * N_HEADS_A
    prow = k_refs[0].shape[0]

    q8 = q_ref[...]
    qh = q8.astype(BF16)
    if hp:
        q_used = q8
        q_lhs = _stack_split(q8)
    else:
        q_used = qh.astype(F32)
    row = lax.broadcasted_iota(jnp.int32, (rows, prow), 0)
    col = lax.broadcasted_iota(jnp.int32, (rows, prow), 1)
    own = (col % N_HEADS_A) == (row // 2)

    @pl.when(s_id == 0)
    def _():
        m_ref[...] = jnp.full(m_ref.shape, NEG_BIG, F32)
        l_ref[...] = jnp.zeros(l_ref.shape, F32)
        acc_ref[...] = jnp.zeros(acc_ref.shape, F32)

    def scores(k):
        if not hp:
            s = _dot_nt(qh, k.astype(BF16))
        else:
            kh, kl = _split(k)
            top = _dot_nt(q_lhs, kh)
            s = top[0:rows] + top[rows:2 * rows] + _dot_nt(qh, kl)
        return jnp.where(own, s, NEG_BIG)

    s = jnp.concatenate([scores(k_refs[j][...]) for j in range(pps)], axis=1)
    m_old = m_ref[...]
    m_new = jnp.maximum(m_old, jnp.max(s, axis=1, keepdims=True))
    alpha = jnp.exp(m_old - m_new)
    p = jnp.exp(s - m_new)
    l_ref[...] = alpha * l_ref[...] + jnp.sum(p, axis=1, keepdims=True)
    ph = p.astype(BF16)
    if hp:
        p_lhs = _stack_split(p)
    pv = jnp.zeros((rows, HEAD_V), F32)
    for j in range(pps):
        cols = slice(j * prow, (j + 1) * prow)
        v = v_refs[j][...]
        if hp:
            vh, vl = _split(v)
            top = _dot(p_lhs[:, cols], vh)
            pv = pv + top[0:rows] + top[rows:2 * rows] + _dot(ph[:, cols], vl)
        else:
            pv = pv + _dot(ph[:, cols], v.astype(BF16))
    acc_ref[...] = alpha * acc_ref[...] + pv
    m_ref[...] = m_new

    @pl.when(s_id == pl.num_programs(1) - 1)
    def _():
        s_new = jnp.sum(q_used * kn_ref[...], axis=1, keepdims=True)
        m_old = m_ref[...]
        m_fin = jnp.maximum(m_old, s_new)
        alpha = jnp.exp(m_old - m_fin)
        p_new = jnp.exp(s_new - m_fin)
        l_fin = alpha * l_ref[...] + p_new
        r_ref[...] = (alpha * acc_ref[...] + p_new * vn_ref[...]) / l_fin
        lam = _lambda(lam_ref[...], lam_init)
        o = (r_ref[pl.ds(0, N_HEADS_A, stride=2), :]
             - lam * r_ref[pl.ds(1, N_HEADS_A, stride=2), :])
        o_ref[...] = _rms(o, og_ref[...]) * (1.0 - lam_init)


def _decode_attention(page_table, q, k_new, v_new, lam_p, og, cache_k, cache_v, layer_slot,
                      lam_init, hp):
    db, n_pages = page_table.shape
    pps = PAGES_PER_STEP
    prow = cache_k.shape[2]
    rows = 2 * N_HEADS_A
    half = jnp.arange(HEAD_V) // HEAD_DIM
    map_mask = (half[None, :] == jnp.arange(2)[:, None]).astype(F32)
    q8 = (q.astype(F32).reshape(db, N_HEADS_A, 1, HEAD_V) * map_mask).reshape(db, rows, HEAD_V)
    k8 = jnp.repeat(k_new.reshape(db, N_HEADS_A, HEAD_V), 2, axis=1)
    v8 = jnp.repeat(v_new.reshape(db, N_HEADS_A, HEAD_V), 2, axis=1)
    kernel = functools.partial(_decode_kernel, pps=pps, lam_init=lam_init, hp=hp)
    tok = pl.BlockSpec((None, rows, HEAD_V), lambda b, s, pt: (b, 0, 0))

    def page_spec(j):
        return pl.BlockSpec((None, None, prow, HEAD_V),
                            lambda b, s, pt: (layer_slot, pt[b * n_pages + s * pps + j], 0, 0))

    grid_spec = pltpu.PrefetchScalarGridSpec(
        num_scalar_prefetch=1,
        grid=(db, n_pages // pps),
        in_specs=[tok, tok, tok,
                  pl.BlockSpec((4, HEAD_DIM), lambda b, s, pt: (0, 0)),
                  pl.BlockSpec((1, HEAD_V), lambda b, s, pt: (0, 0))]
                 + [page_spec(j) for j in range(pps)] * 2,
        out_specs=pl.BlockSpec((None, N_HEADS_A, HEAD_V), lambda b, s, pt: (b, 0, 0)),
        scratch_shapes=[pltpu.VMEM((rows, 1), F32), pltpu.VMEM((rows, 1), F32),
                        pltpu.VMEM((rows, HEAD_V), F32), pltpu.VMEM((rows, HEAD_V), F32)],
    )
    out = pl.pallas_call(
        kernel,
        grid_spec=grid_spec,
        out_shape=jax.ShapeDtypeStruct((db, N_HEADS_A, HEAD_V), F32),
        compiler_params=_params(("parallel", "arbitrary")),
        name="decode_attn",
    )(page_table.reshape(-1), q8, k8, v8, lam_p, og, *([cache_k] * pps), *([cache_v] * pps))
    return out.reshape(db, D_A)


def _conv_tail(conv, lng, lnb, o, x, w_ref, hp=False):
    y = _layer_norm(conv, lng, lnb)
    c = y * jax.nn.sigmoid(y)
    return (x + _mm(o, w_ref[0:D_A, :], hp) + _mm(c, w_ref[D_A:D_A + D_B, :], hp))


def _even_out_kernel(hb_ref, hprev_ref, o_ref, x_ref, wdw_ref, bdw_ref, lng_ref, lnb_ref,
                     w_ref, y_ref, win_ref, conv_ref, *, tm, tiles_per_seq):
    i = pl.program_id(0)
    first = (i % tiles_per_seq) == 0
    win_ref[0, 0:CONV_HALO, :] = jnp.where(first, 0.0, hprev_ref[...])
    win_ref[0, CONV_HALO:CONV_HALO + tm, :] = hb_ref[...]
    lead = CONV_HALO - (CONV_WIDTH - 1)
    shifted_rows = tm + CONV_HALO - SUBLANES
    for s in range(1, SUBLANES):
        win_ref[s, 0:shifted_rows, :] = win_ref[0, s:s + shifted_rows, :]
    bias = bdw_ref[...]
    for r0 in range(0, tm, CONV_ROWS):
        acc = jnp.broadcast_to(bias, (CONV_ROWS, D_B))
        for j in range(CONV_WIDTH):
            s = (lead + j) % SUBLANES
            a = r0 + lead + j - s
            acc = acc + win_ref[s, a:a + CONV_ROWS, :] * wdw_ref[j:j + 1, :]
        conv_ref[r0:r0 + CONV_ROWS, :] = acc
    y_ref[...] = _conv_tail(conv_ref[...], lng_ref[...], lnb_ref[...], o_ref[...], x_ref[...],
                            w_ref)


def _even_out(hb, o, x, wdw, bdw, lng, lnb, w_out, tm, seq):
    T = x.shape[0]
    tiles_per_seq = seq // tm
    halo_blocks = tm // CONV_HALO
    row = lambda i: (i, 0)
    fix = lambda i: (0, 0)
    kernel = functools.partial(_even_out_kernel, tm=tm, tiles_per_seq=tiles_per_seq)
    return pl.pallas_call(
        kernel,
        grid=(T // tm,),
        in_specs=[pl.BlockSpec((tm, D_B), row),
                  pl.BlockSpec((CONV_HALO, D_B), lambda i: (jnp.maximum(i * halo_blocks - 1, 0), 0)),
                  pl.BlockSpec((tm, D_A), row),
                  pl.BlockSpec((tm, D_MODEL), row),
                  pl.BlockSpec((CONV_HALO, D_B), fix),
                  pl.BlockSpec((1, D_B), fix),
                  pl.BlockSpec((1, D_B), fix),
                  pl.BlockSpec((1, D_B), fix),
                  pl.BlockSpec((D_A + D_B, D_MODEL), fix)],
        out_specs=pl.BlockSpec((tm, D_MODEL), row),
        out_shape=jax.ShapeDtypeStruct((T, D_MODEL), F32),
        scratch_shapes=[pltpu.VMEM((SUBLANES, CONV_HALO + tm, D_B), F32),
                        pltpu.VMEM((tm, D_B), F32)],
        compiler_params=_params(("parallel",)),
        name="even_out",
    )(hb, hb, o, x, wdw, bdw, lng, lnb, w_out)


def _even_out_sample_kernel(st_ref, hb_ref, o_ref, x_ref, wdw_ref, bdw_ref, lng_ref, lnb_ref,
                            w_ref, y_ref, *, hp):
    w = wdw_ref[...]
    conv = jnp.sum(st_ref[...] * w[0:CONV_WIDTH - 1, :][None], axis=1)
    conv = conv + hb_ref[...] * w[CONV_WIDTH - 1:CONV_WIDTH, :] + bdw_ref[...]
    y_ref[...] = _conv_tail(conv, lng_ref[...], lnb_ref[...], o_ref[...], x_ref[...], w_ref,
                            hp)


def _even_out_sample(state, hb, o, x, wdw, bdw, lng, lnb, w_out, hp):
    db = x.shape[0]
    return pl.pallas_call(
        functools.partial(_even_out_sample_kernel, hp=hp),
        out_shape=jax.ShapeDtypeStruct((db, D_MODEL), F32),
        compiler_params=pltpu.CompilerParams(vmem_limit_bytes=VMEM_LIMIT),
        name="even_out_sample",
    )(state, hb, o, x, wdw, bdw, lng, lnb, w_out)


def _odd_front(x, g_ref, wuv_ref, lng_ref, lnb_ref, hp=False):
    h = _rms(x, g_ref[...])
    z = _mm(h, wuv_ref[...], hp)
    z = 0.5 * z * (1.0 + lax.erf(z * (1.0 / math.sqrt(2.0))))
    u = z[:, 0:D_C]
    v = _layer_norm(z[:, D_C:2 * D_C], lng_ref[...], lnb_ref[...])
    return u, v


def _odd_kernel(x_ref, g_ref, wuv_ref, lng_ref, lnb_ref, ws_ref, bs_ref, wo_ref, y_ref,
                sv_ref, *, tm):
    r = lax.broadcasted_iota(jnp.int32, (CHUNK, CHUNK), 0)
    c = lax.broadcasted_iota(jnp.int32, (CHUNK, CHUNK), 1)
    gw = D_C // N_GROUPS_C
    wsg = [jnp.where(c <= r, ws_ref[g], 0.0).astype(BF16)
           for g in range(N_GROUPS_C)]
    slab = tm // ODD_SPLIT
    for part in range(ODD_SPLIT):
        sl = slice(part * slab, (part + 1) * slab)
        x = x_ref[sl, :]
        u, v = _odd_front(x, g_ref, wuv_ref, lng_ref, lnb_ref)
        vb = v.astype(BF16)
        for g in range(N_GROUPS_C):
            cols = slice(g * gw, (g + 1) * gw)
            for ch in range(slab // CHUNK):
                rows = slice(ch * CHUNK, (ch + 1) * CHUNK)
                dst = slice(part * slab + ch * CHUNK, part * slab + (ch + 1) * CHUNK)
                sv_ref[dst, cols] = _dot(wsg[g], vb[rows, cols]) + bs_ref[:, cols]
        y_ref[sl, :] = x + _dot((u * sv_ref[sl, :]).astype(BF16), wo_ref[...])


def _odd(x, g, wuv, lng, lnb, ws, bs, wo, tm):
    T = x.shape[0]
    row = lambda i: (i, 0)
    fix = lambda i: (0, 0)
    return pl.pallas_call(
        functools.partial(_odd_kernel, tm=tm),
        grid=(T // tm,),
        in_specs=[pl.BlockSpec((tm, D_MODEL), row),
                  pl.BlockSpec((1, D_MODEL), fix),
                  pl.BlockSpec((D_MODEL, 2 * D_C), fix),
                  pl.BlockSpec((1, D_C), fix),
                  pl.BlockSpec((1, D_C), fix),
                  pl.BlockSpec((N_GROUPS_C, CHUNK, CHUNK), lambda i: (0, 0, 0)),
                  pl.BlockSpec((CHUNK, D_C), fix),
                  pl.BlockSpec((D_C, D_MODEL), fix)],
        out_specs=pl.BlockSpec((tm, D_MODEL), row),
        out_shape=jax.ShapeDtypeStruct((T, D_MODEL), F32),
        scratch_shapes=[pltpu.VMEM((tm, D_C), F32)],
        compiler_params=_params(("parallel",)),
        name="odd_mixer",
    )(x, g, wuv, lng, lnb, ws, bs, wo)


def _odd_sample_kernel(x_ref, g_ref, wuv_ref, lng_ref, lnb_ref, ws0_ref, bs0_ref, wo_ref,
                       y_ref, v_ref, *, hp):
    u, v = _odd_front(x_ref[...], g_ref, wuv_ref, lng_ref, lnb_ref, hp)
    v_ref[...] = v
    sv = v * ws0_ref[...] + bs0_ref[...]
    y_ref[...] = x_ref[...] + _mm(u * sv, wo_ref[...], hp)


def _odd_sample(x, g, wuv, lng, lnb, ws0, bs0, wo, hp):
    db = x.shape[0]
    return pl.pallas_call(
        functools.partial(_odd_sample_kernel, hp=hp),
        out_shape=[jax.ShapeDtypeStruct((db, D_MODEL), F32),
                   jax.ShapeDtypeStruct((db, D_C), F32)],
        compiler_params=pltpu.CompilerParams(vmem_limit_bytes=VMEM_LIMIT),
        name="odd_mixer_sample",
    )(x, g, wuv, lng, lnb, ws0, bs0, wo)


def _route(logits):
    lane = lax.broadcasted_iota(jnp.int32, logits.shape, 1)
    lane_f = lane.astype(F32)
    none = float(ROUTER_LANES)
    is_g = lane < N_EXPERT_GROUPS
    gl = jnp.where(is_g, logits, NEG_BIG)
    gmax = jnp.max(gl, axis=1, keepdims=True)
    gidx = jnp.min(jnp.where(is_g & (gl == gmax), lane_f, none), axis=1, keepdims=True)
    gw = 1.0 / jnp.sum(jnp.where(is_g, jnp.exp(gl - gmax), 0.0), axis=1, keepdims=True)
    e_f = lane_f - float(N_EXPERT_GROUPS)
    is_e = (lane >= N_EXPERT_GROUPS) & (lane < N_EXPERT_GROUPS + N_EXPERTS)
    grp_lo = gidx * float(EXPERTS_PER_GROUP)
    in_grp = is_e & (e_f >= grp_lo) & (e_f < grp_lo + float(EXPERTS_PER_GROUP))
    cand = jnp.where(in_grp, logits, NEG_BIG)
    v1 = jnp.max(cand, axis=1, keepdims=True)
    i1 = jnp.min(jnp.where(in_grp & (cand == v1), lane_f, none), axis=1, keepdims=True)
    rest = in_grp & (lane_f != i1)
    cand2 = jnp.where(rest, logits, NEG_BIG)
    v2 = jnp.max(cand2, axis=1, keepdims=True)
    i2 = jnp.min(jnp.where(rest & (cand2 == v2), lane_f, none), axis=1, keepdims=True)
    t = jnp.exp(v2 - v1)
    p1 = 1.0 / (1.0 + t)
    p2 = t * p1
    return jnp.where(lane_f == i1, p1 * gw, jnp.where(lane_f == i2, p2 * gw, 0.0))


def _moe_kernel(x_ref, g_ref, wr_ref, br_ref, wup_ref, wdn_ref, y_ref, xn_ref, gate_ref,
                acc_ref, *, hp):
    grp = pl.program_id(1)

    @pl.when(grp == 0)
    def _():
        xn = _rms(x_ref[...], g_ref[...]).astype(xn_ref.dtype)
        xn_ref[...] = xn
        gate_ref[...] = _route(_mm(xn, wr_ref[...], hp) + br_ref[...])
        acc_ref[...] = jnp.zeros(acc_ref.shape, F32)

    xn = xn_ref[...]
    gate = gate_ref[...]
    lane = lax.broadcasted_iota(jnp.int32, gate.shape, 1)
    acc = acc_ref[...]
    for j in range(EXPERTS_PER_GROUP):
        e_lane = N_EXPERT_GROUPS + grp * EXPERTS_PER_GROUP + j
        ge = jnp.sum(jnp.where(lane == e_lane, gate, 0.0), axis=1, keepdims=True)
        hu = _mm(xn, wup_ref[j], hp)
        a = hu[:, 0:D_FF_EXPERT]
        b = hu[:, D_FF_EXPERT:]
        act = a * jax.nn.sigmoid(a) * b * ge
        acc = acc + _mm(act, wdn_ref[j], hp)
    acc_ref[...] = acc

    @pl.when(grp == pl.num_programs(1) - 1)
    def _():
        y_ref[...] = x_ref[...] + acc_ref[...]


def _moe(x, g, wr, br, wup, wdn, layer, tm, hp=False):
    T = x.shape[0]
    row = lambda i, e: (i, 0)
    fix = lambda i, e: (0, 0)
    return pl.pallas_call(
        functools.partial(_moe_kernel, hp=hp),
        grid=(T // tm, N_EXPERT_GROUPS),
        in_specs=[pl.BlockSpec((tm, D_MODEL), row),
                  pl.BlockSpec((1, D_MODEL), fix),
                  pl.BlockSpec((D_MODEL, ROUTER_LANES), fix),
                  pl.BlockSpec((1, ROUTER_LANES), fix),
                  pl.BlockSpec((None, EXPERTS_PER_GROUP, D_MODEL, 2 * D_FF_EXPERT),
                               lambda i, e: (layer, e, 0, 0)),
                  pl.BlockSpec((None, EXPERTS_PER_GROUP, D_FF_EXPERT, D_MODEL),
                               lambda i, e: (layer, e, 0, 0))],
        out_specs=pl.BlockSpec((tm, D_MODEL), row),
        out_shape=jax.ShapeDtypeStruct((T, D_MODEL), F32),
        scratch_shapes=[pltpu.VMEM((tm, D_MODEL), F32 if hp else BF16),
                        pltpu.VMEM((tm, ROUTER_LANES), F32),
                        pltpu.VMEM((tm, D_MODEL), F32)],
        compiler_params=_params(("parallel", "arbitrary"), MOE_VMEM_LIMIT),
        name="hier_moe",
    )(x, g, wr, br, wup, wdn)


def _rope_tables(pos):
    half = HEAD_DIM // 2
    inv = ROPE_THETA ** (-jnp.arange(half, dtype=F32) / half)
    ang = pos.astype(F32)[:, None] * inv[None, :]
    cos, sin = jnp.cos(ang), jnp.sin(ang)
    return (jnp.concatenate([cos, cos, cos, cos], axis=1),
            jnp.concatenate([-sin, sin, -sin, sin], axis=1))


def _tile_gain(g, reps):
    return jnp.tile(g.astype(F32), reps)[None, :]


def kernel(x_prompt, x_sample, cache_k, cache_v, state_conv, page_table, norm_mix, norm_ffn, w_in_even, w_out_even, q_norm, k_norm, lambda_qk, head_norm, w_dw, b_dw, conv_ln_g, conv_ln_b, w_uv, v_ln_g, v_ln_b, w_spatial, b_spatial, w_out_odd, w_router_group, b_router_group, w_router_expert, b_router_expert, w_expert_up, w_expert_down):
    batch, seq, _ = x_prompt.shape
    db = x_sample.shape[0]
    n_pages = page_table.shape[1]
    page = cache_k.shape[2]
    past_len = n_pages * page
    tm = TILE_TOKENS

    xp = x_prompt.reshape(batch * seq, D_MODEL)
    xs = x_sample.reshape(db, D_MODEL)
    cos_p, sin_p = _rope_tables(jnp.arange(seq))
    cos_s, sin_s = _rope_tables(jnp.full((db,), past_len))
    ck = cache_k.reshape(cache_k.shape[0], cache_k.shape[1], page * N_HEADS_A, HEAD_V)
    cv = cache_v.reshape(cache_v.shape[0], cache_v.shape[1], page * N_HEADS_A, HEAD_V)
    group_of = jnp.arange(D_A) // HEAD_DIM
    gmat32 = jnp.where(group_of[:, None] == group_of[None, :], 1.0 / HEAD_DIM, 0.0).astype(F32)
    gmat = gmat32.astype(BF16)
    w_in_bf = w_in_even.astype(BF16)

    cp_l, cs_l, chs_l = [], [], []
    n_even = w_in_even.shape[0]
    kv_p = kv_s = None
    for layer in range(DEPTH):
        i = layer // 2
        g_mix = norm_mix[layer][None, :]
        hp = layer < HP_SAMPLE_LAYERS
        if layer % 2 == 0:
            lam_init = 0.8 - 0.6 * math.exp(-0.3 * layer)
            w_out = w_out_even[i].astype(BF16)
            qg = _tile_gain(q_norm[i], D_A // HEAD_DIM)
            kg = _tile_gain(k_norm[i], D_A // HEAD_DIM)
            og = head_norm[i][None, :]
            wdw = jnp.pad(w_dw[i], ((0, CONV_HALO - CONV_WIDTH), (0, 0)))
            bdw, lng, lnb = b_dw[i][None, :], conv_ln_g[i][None, :], conv_ln_b[i][None, :]
            q, kp_all, kb, vp_all, vb, hb = _even_in(xp, g_mix, w_in_bf, i, n_even, kv_p, qg, kg,
                                                     gmat, cos_p, sin_p, tm, seq // tm)
            kv_p = (kp_all, vp_all)
            o = _prompt_attention(q, kb, vb, lambda_qk[i], og, batch, seq, lam_init)
            xp = _even_out(hb, o, xp, wdw, bdw, lng, lnb, w_out, tm, seq)
            cp_l.append(hb.reshape(batch, seq, D_B)[:, seq - (CONV_WIDTH - 1):, :])
            qs, ks_all, _, vs_all, _, hbs = _even_in(
                xs, g_mix, w_in_even if hp else w_in_bf, i, n_even, kv_s, qg, kg,
                gmat32 if hp else gmat, cos_s, sin_s, db, 1, hp)
            kv_s = (ks_all, vs_all)
            os_ = _decode_attention(page_table, qs, ks_all[i], vs_all[i], lambda_qk[i], og, ck, cv,
                                    i, lam_init, hp)
            xs = _even_out_sample(state_conv[i], hbs, os_, xs, wdw, bdw, lng,
                                  lnb, w_out_even[i] if hp else w_out, hp)
            cs_l.append(jnp.concatenate([state_conv[i][:, 1:, :], hbs[:, None, :]], axis=1))
        else:
            wuv = w_uv[i].astype(BF16)
            wo = w_out_odd[i].astype(BF16)
            lng, lnb = v_ln_g[i][None, :], v_ln_b[i][None, :]
            gw = D_C // N_GROUPS_C
            bs = jnp.repeat(b_spatial[i].T, gw, axis=1)
            xp = _odd(xp, g_mix, wuv, lng, lnb, w_spatial[i], bs, wo, tm)
            ws0 = jnp.repeat(w_spatial[i][:, 0, 0], gw)[None, :]
            xs, vch = _odd_sample(xs, g_mix, w_uv[i] if hp else wuv, lng, lnb, ws0, bs[0:1, :],
                                  w_out_odd[i] if hp else wo, hp)
            chs_l.append(vch.reshape(db, 1, D_C))
        g_ffn = norm_ffn[layer][None, :]
        pad = ROUTER_LANES - N_EXPERT_GROUPS - N_EXPERTS
        wr32 = jnp.pad(jnp.concatenate([w_router_group[layer], w_router_expert[layer]], axis=1),
                       ((0, 0), (0, pad)))
        wr = wr32.astype(BF16)
        br = jnp.pad(jnp.concatenate([b_router_group[layer], b_router_expert[layer]]),
                     (0, pad))[None, :]
        xp = _moe(xp, g_ffn, wr, br, w_expert_up, w_expert_down, layer, MOE_TILE_TOKENS)
        xs = _moe(xs, g_ffn, wr32 if hp else wr, br, w_expert_up, w_expert_down, layer, db, hp)

    return (xp.reshape(batch, seq, D_MODEL), xs.reshape(db, 1, D_MODEL),
            kv_p[0].reshape(n_even, batch, seq, N_HEADS_A, HEAD_V),
            kv_p[1].reshape(n_even, batch, seq, N_HEADS_A, HEAD_V),
            kv_s[0].reshape(n_even, db, 1, N_HEADS_A, HEAD_V),
            kv_s[1].reshape(n_even, db, 1, N_HEADS_A, HEAD_V),
            jnp.stack(cp_l), jnp.stack(cs_l), jnp.stack(chs_l))
```

```python
import functools
import math

import jax
import jax.numpy as jnp
from jax import lax
from jax.experimental import pallas as pl
from jax.experimental.pallas import tpu as pltpu

F32 = jnp.float32
BF16 = jnp.bfloat16

D_MODEL = 1024
DEPTH = 4
N_HEADS_A = 4
HEAD_DIM = 64
HEAD_V = 2 * HEAD_DIM
D_A = N_HEADS_A * HEAD_V
D_B = D_MODEL // 2
CONV_WIDTH = 31
CONV_HALO = 32
D_IN_EVEN = 3 * D_A + 2 * D_B
D_C = D_MODEL
N_GROUPS_C = 4
CHUNK = 128
N_EXPERT_GROUPS = 4
EXPERTS_PER_GROUP = 4
N_EXPERTS = N_EXPERT_GROUPS * EXPERTS_PER_GROUP
D_FF_EXPERT = 256
ROPE_THETA = 10000.0
EPS = 1e-6
NEG_BIG = -1e30
LANES = 128
SUBLANES = 8
ROUTER_LANES = LANES

VMEM_LIMIT = 52 * 1024 * 1024
MOE_VMEM_LIMIT = 58 * 1024 * 1024

TILE_TOKENS = 1024
CONV_TILE_TOKENS = 512
EVEN_IN_SPLIT = 2
ODD_SPLIT = 1
MOE_TILE_TOKENS = 1024
ATTN_BLOCK = 512
ATTN_SUB = 256
ATTN_UNROLL = 4
CONV_ROWS = 256
PAGES_PER_STEP = 32
HP_SAMPLE_LAYERS = 2


def _params(sem, limit=None):
    return pltpu.CompilerParams(dimension_semantics=sem, vmem_limit_bytes=limit or VMEM_LIMIT)


def _rms(x, g):
    return x * lax.rsqrt(jnp.mean(x * x, axis=-1, keepdims=True) + EPS) * g


def _layer_norm(x, g, b):
    mu = jnp.mean(x, axis=-1, keepdims=True)
    xc = x - mu
    var = jnp.mean(xc * xc, axis=-1, keepdims=True)
    return xc * lax.rsqrt(var + EPS) * g + b


def _dot(a, b):
    return jnp.dot(a, b, preferred_element_type=F32)


def _dot_nt(a, b):
    return lax.dot_general(a, b, (((1,), (1,)), ((), ())), preferred_element_type=F32)


def _split(x):
    hi = x.astype(BF16)
    return hi, (x - hi.astype(F32)).astype(BF16)


def _stack_split(x):
    hi = x.astype(BF16).astype(F32)
    return jnp.concatenate([hi, x - hi], axis=0).astype(BF16)


def _mm(a, w, hp):
    if not hp:
        return _dot(a.astype(BF16), w.astype(BF16))
    m = a.shape[0]
    wh, wl = _split(w)
    top = _dot(_stack_split(a), wh)
    return top[0:m] + top[m:2 * m] + _dot(a.astype(BF16), wl)


def _lambda(lp, lam_init):
    a = jnp.sum(lp[0:1] * lp[1:2], axis=1, keepdims=True)
    b = jnp.sum(lp[2:3] * lp[3:4], axis=1, keepdims=True)
    return jnp.exp(a) - jnp.exp(b) + lam_init


def _even_in_kernel(x_ref, g_ref, w_ref, qg_ref, kg_ref, gmat_ref, cos_ref, sin_ref, *rest,
                    hp, slot, first):
    q_ref, k32_ref, kb_ref, v32_ref, vb_ref, hb_ref = rest[-6:]
    tm = x_ref.shape[0]
    if first:
        for other in range(k32_ref.shape[0]):
            if other != slot:
                k32_ref[other] = jnp.zeros(k32_ref.shape[1:], F32)
                v32_ref[other] = jnp.zeros(v32_ref.shape[1:], F32)
        k_out, v_out = k32_ref.at[slot], v32_ref.at[slot]
    else:
        k_out, v_out = k32_ref, v32_ref
    lane = lax.broadcasted_iota(jnp.int32, (tm // EVEN_IN_SPLIT, D_A), 1)
    first_half = (lane % HEAD_DIM) < (HEAD_DIM // 2)

    for part in range(EVEN_IN_SPLIT):
        rows = tm // EVEN_IN_SPLIT
        r0 = part * rows
        sl = slice(r0, r0 + rows)
        h = _rms(x_ref[sl, :], g_ref[...])
        proj = _mm(h, w_ref[...], hp)
        cos = jnp.concatenate([cos_ref[sl, :]] * (D_A // LANES), axis=1)
        sin = jnp.concatenate([sin_ref[sl, :]] * (D_A // LANES), axis=1)

        def norm_rope(z, gain):
            ms = _mm(z * z, gmat_ref[...], hp)
            zn = z * lax.rsqrt(ms + EPS) * gain
            rot = jnp.where(first_half,
                            pltpu.roll(zn, D_A - HEAD_DIM // 2, 1),
                            pltpu.roll(zn, HEAD_DIM // 2, 1))
            return zn * cos + rot * sin

        q = norm_rope(proj[:, 0:D_A], qg_ref[...])
        k = norm_rope(proj[:, D_A:2 * D_A], kg_ref[...])
        v = proj[:, 2 * D_A:3 * D_A]
        a = proj[:, 3 * D_A:3 * D_A + D_B]
        g = proj[:, 3 * D_A + D_B:]
        q_ref[sl, :] = (q * (1.0 / math.sqrt(HEAD_DIM))).astype(q_ref.dtype)
        kb_ref[sl, :] = k.astype(BF16)
        vb_ref[sl, :] = v.astype(BF16)
        hb_ref[sl, :] = a * jax.nn.sigmoid(g)
        for hd in range(N_HEADS_A):
            cols = slice(hd * HEAD_V, (hd + 1) * HEAD_V)
            dst = pl.ds(r0 * N_HEADS_A + hd, rows, stride=N_HEADS_A)
            k_out[dst, :] = k[:, cols]
            v_out[dst, :] = v[:, cols]


def _even_in(x, g, w_in, slot, n_slots, kv_prev, qg, kg, gmat, cos, sin, tm, pos_blocks, hp=False):
    T = x.shape[0]
    row = lambda i: (i, 0)
    fix = lambda i: (0, 0)
    pos = lambda i: (i % pos_blocks, 0)
    stacked = jax.ShapeDtypeStruct((n_slots, T * N_HEADS_A, HEAD_V), F32)
    outs = [jax.ShapeDtypeStruct((T, D_A), F32 if hp else BF16),
            stacked,
            jax.ShapeDtypeStruct((T, D_A), BF16),
            stacked,
            jax.ShapeDtypeStruct((T, D_A), BF16),
            jax.ShapeDtypeStruct((T, D_B), F32)]
    wide = pl.BlockSpec((tm, D_A), row)
    first = kv_prev is None
    if first:
        tall = pl.BlockSpec((n_slots, tm * N_HEADS_A, HEAD_V), lambda i: (0, i, 0))
        extra_specs, extra_args, aliases = [], [], {}
    else:
        tall = pl.BlockSpec((None, tm * N_HEADS_A, HEAD_V), lambda i: (slot, i, 0))
        extra_specs = [pl.BlockSpec(memory_space=pl.ANY)] * 2
        extra_args = list(kv_prev)
        aliases = {8: 1, 9: 3}
    return pl.pallas_call(
        functools.partial(_even_in_kernel, hp=hp, slot=slot, first=first),
        grid=(T // tm,),
        in_specs=[pl.BlockSpec((tm, D_MODEL), row),
                  pl.BlockSpec((1, D_MODEL), fix),
                  pl.BlockSpec((None, D_MODEL, D_IN_EVEN), lambda i: (slot, 0, 0)),
                  pl.BlockSpec((1, D_A), fix),
                  pl.BlockSpec((1, D_A), fix),
                  pl.BlockSpec((D_A, D_A), fix),
                  pl.BlockSpec((tm, LANES), pos),
                  pl.BlockSpec((tm, LANES), pos)] + extra_specs,
        out_specs=[wide, tall, wide, tall, wide, pl.BlockSpec((tm, D_B), row)],
        out_shape=outs,
        input_output_aliases=aliases,
        compiler_params=_params(("parallel",)),
        name="even_in",
    )(x, g, w_in, qg, kg, gmat, cos, sin, *extra_args)


def _attn_kernel(q_ref, k_ref, v_ref, lam_ref, og_ref, o_ref, v1_ref, sa_ref, sb_ref, m_ref,
                 acc_ref, *, blk, sub, lam_init):
    qi = pl.program_id(2)
    per_blk = blk // sub
    assert per_blk % 2 == 0

    @pl.when(qi == 0)
    def _():
        v1_ref[:, 0:HEAD_V] = v_ref[...]
        v1_ref[:, HEAD_V:2 * HEAD_V] = jnp.ones((v1_ref.shape[0], HEAD_V), BF16)

    q = q_ref[...]
    lane = lax.broadcasted_iota(jnp.int32, q.shape, 1)
    zero = jnp.zeros_like(q)
    qq = jnp.concatenate([jnp.where(lane < HEAD_DIM, q, zero),
                          jnp.where(lane >= HEAD_DIM, q, zero)], axis=0)
    m_ref[...] = jnp.full(m_ref.shape, NEG_BIG, F32)
    acc_ref[...] = jnp.zeros(acc_ref.shape, F32)

    def scores(start):
        if not isinstance(start, int):
            start = pl.multiple_of(start, sub)
        return _dot_nt(qq, k_ref[pl.ds(start, sub), :])

    def consume(s, start, mask):
        vblk = v1_ref[pl.ds(start, sub), :]
        if mask is not None:
            s = jnp.where(mask, s, NEG_BIG)
        m_old = m_ref[...]
        m_new = jnp.maximum(m_old, jnp.max(s, axis=1, keepdims=True))
        alpha = jnp.exp(m_old - m_new)
        p = jnp.exp((s - jnp.concatenate([m_new] * (sub // LANES), axis=1)).astype(BF16))
        acc_ref[...] = (jnp.concatenate([alpha, alpha], axis=1) * acc_ref[...]
                        + _dot(p, vblk))
        m_ref[...] = m_new

    bufs = (sa_ref, sb_ref)
    sa_ref[...] = scores(0)

    def full_blocks(first, count):
        for u in range(count * per_blk):
            start = pl.multiple_of(first * blk + u * sub, sub)
            bufs[(u + 1) % 2][...] = scores(start + sub)
            consume(bufs[u % 2][...], start, None)

    def body(t, carry):
        full_blocks(ATTN_UNROLL * t, ATTN_UNROLL)
        return carry

    trips = qi // ATTN_UNROLL
    lax.fori_loop(0, trips, body, 0)
    done = trips * ATTN_UNROLL
    size = ATTN_UNROLL // 2
    while size >= 1:
        take = ((qi - done) // size) % 2 == 1

        @pl.when(take)
        def _(done=done, size=size):
            full_blocks(done, size)

        done = done + jnp.where(take, size, 0)
        size //= 2

    r = lax.broadcasted_iota(jnp.int32, (2 * blk, sub), 0)
    c = lax.broadcasted_iota(jnp.int32, (2 * blk, sub), 1)
    r = jnp.where(r >= blk, r - blk, r)
    for u in range(per_blk):
        start = pl.multiple_of(qi * blk + u * sub, sub)
        if u + 1 < per_blk:
            bufs[(u + 1) % 2][...] = scores(start + sub)
        consume(bufs[u % 2][...], start, c + u * sub <= r)

    lam = _lambda(lam_ref[...], lam_init)
    acc = acc_ref[...]
    ratio = acc[:, 0:HEAD_V] / acc[:, HEAD_V:2 * HEAD_V]
    o = ratio[0:blk] - lam * ratio[blk:2 * blk]
    o_ref[...] = (_rms(o, og_ref[...]) * (1.0 - lam_init)).astype(o_ref.dtype)


def _prompt_attention(q, kb, vb, lam_p, og, batch, seq, lam_init):
    blk = ATTN_BLOCK
    nq = seq // blk
    sub = ATTN_SUB
    kernel = functools.partial(_attn_kernel, blk=blk, sub=sub, lam_init=lam_init)
    scr = [pltpu.VMEM((seq, 2 * HEAD_V), BF16), pltpu.VMEM((2 * blk, sub), F32),
           pltpu.VMEM((2 * blk, sub), F32), pltpu.VMEM((2 * blk, HEAD_V), F32),
           pltpu.VMEM((2 * blk, 2 * HEAD_V), F32)]
    return pl.pallas_call(
        kernel,
        grid=(batch, N_HEADS_A, nq),
        in_specs=[pl.BlockSpec((blk, HEAD_V), lambda b, h, i: (b * nq + i, h)),
                  pl.BlockSpec((seq, HEAD_V), lambda b, h, i: (b, h)),
                  pl.BlockSpec((seq, HEAD_V), lambda b, h, i: (b, h)),
                  pl.BlockSpec((4, HEAD_DIM), lambda b, h, i: (0, 0)),
                  pl.BlockSpec((1, HEAD_V), lambda b, h, i: (0, 0))],
        out_specs=pl.BlockSpec((blk, HEAD_V), lambda b, h, i: (b * nq + i, h)),
        out_shape=jax.ShapeDtypeStruct((batch * seq, D_A), BF16),
        scratch_shapes=scr,
        compiler_params=_params(("parallel", "parallel", "arbitrary")),
        name="prompt_attn",
    )(q, kb, vb, lam_p, og)


def _decode_kernel(pt_ref, q_ref, kn_ref, vn_ref, lam_ref, og_ref, *rest, pps, lam_init, hp):
    k_refs = rest[:pps]
    v_refs = rest[pps:2 * pps]
    o_ref = rest[2 * pps]
    m_ref, l_ref, acc_ref, r_ref = rest[2 * pps + 1:]
    s_id = pl.program_id(1)
    rows = 2 * N_HEADS_A
    prow = k_refs[0].shape[0]

    q8 = q_ref[...]
    qh = q8.astype(BF16)
    if hp:
        q_used = q8
        q_lhs = _stack_split(q8)
    else:
        q_used = qh.astype(F32)
    row = lax.broadcasted_iota(jnp.int32, (rows, prow), 0)
    col = lax.broadcasted_iota(jnp.int32, (rows, prow), 1)
    own = (col % N_HEADS_A) == (row // 2)

    @pl.when(s_id == 0)
    def _():
        m_ref[...] = jnp.full(m_ref.shape, NEG_BIG, F32)
        l_ref[...] = jnp.zeros(l_ref.shape, F32)
        acc_ref[...] = jnp.zeros(acc_ref.shape, F32)

    def scores(k):
        if not hp:
            s = _dot_nt(qh, k.astype(BF16))
        else:
            kh, kl = _split(k)
            top = _dot_nt(q_lhs, kh)
            s = top[0:rows] + top[rows:2 * rows] + _dot_nt(qh, kl)
        return jnp.where(own, s, NEG_BIG)

    s = jnp.concatenate([scores(k_refs[j][...]) for j in range(pps)], axis=1)
    m_old = m_ref[...]
    m_new = jnp.maximum(m_old, jnp.max(s, axis=1, keepdims=True))
    alpha = jnp.exp(m_old - m_new)
    p = jnp.exp(s - m_new)
    l_ref[...] = alpha * l_ref[...] + jnp.sum(p, axis=1, keepdims=True)
    ph = p.astype(BF16)
    if hp:
        p_lhs = _stack_split(p)
    pv = jnp.zeros((rows, HEAD_V), F32)
    for j in range(pps):
        cols = slice(j * prow, (j + 1) * prow)
        v = v_refs[j][...]
        if hp:
            vh, vl = _split(v)
            top = _dot(p_lhs[:, cols], vh)
            pv = pv + top[0:rows] + top[rows:2 * rows] + _dot(ph[:, cols], vl)
        else:
            pv = pv + _dot(ph[:, cols], v.astype(BF16))
    acc_ref[...] = alpha * acc_ref[...] + pv
    m_ref[...] = m_new

    @pl.when(s_id == pl.num_programs(1) - 1)
    def _():
        s_new = jnp.sum(q_used * kn_ref[...], axis=1, keepdims=True)
        m_old = m_ref[...]
        m_fin = jnp.maximum(m_old, s_new)
        alpha = jnp.exp(m_old - m_fin)
        p_new = jnp.exp(s_new - m_fin)
        l_fin = alpha * l_ref[...] + p_new
        r_ref[...] = (alpha * acc_ref[...] + p_new * vn_ref[...]) / l_fin
        lam = _lambda(lam_ref[...], lam_init)
        o = (r_ref[pl.ds(0, N_HEADS_A, stride=2), :]
             - lam * r_ref[pl.ds(1, N_HEADS_A, stride=2), :])
        o_ref[...] = _rms(o, og_ref[...]) * (1.0 - lam_init)


def _decode_attention(page_table, q, k_new, v_new, lam_p, og, cache_k, cache_v, layer_slot,
                      lam_init, hp):
    db, n_pages = page_table.shape
    pps = PAGES_PER_STEP
    prow = cache_k.shape[2]
    rows = 2 * N_HEADS_A
    half = jnp.arange(HEAD_V) // HEAD_DIM
    map_mask = (half[None, :] == jnp.arange(2)[:, None]).astype(F32)
    q8 = (q.astype(F32).reshape(db, N_HEADS_A, 1, HEAD_V) * map_mask).reshape(db, rows, HEAD_V)
    k8 = jnp.repeat(k_new.reshape(db, N_HEADS_A, HEAD_V), 2, axis=1)
    v8 = jnp.repeat(v_new.reshape(db, N_HEADS_A, HEAD_V), 2, axis=1)
    kernel = functools.partial(_decode_kernel, pps=pps, lam_init=lam_init, hp=hp)
    tok = pl.BlockSpec((None, rows, HEAD_V), lambda b, s, pt: (b, 0, 0))

    def page_spec(j):
        return pl.BlockSpec((None, None, prow, HEAD_V),
                            lambda b, s, pt: (layer_slot, pt[b * n_pages + s * pps + j], 0, 0))

    grid_spec = pltpu.PrefetchScalarGridSpec(
        num_scalar_prefetch=1,
        grid=(db, n_pages // pps),
        in_specs=[tok, tok, tok,
                  pl.BlockSpec((4, HEAD_DIM), lambda b, s, pt: (0, 0)),
                  pl.BlockSpec((1, HEAD_V), lambda b, s, pt: (0, 0))]
                 + [page_spec(j) for j in range(pps)] * 2,
        out_specs=pl.BlockSpec((None, N_HEADS_A, HEAD_V), lambda b, s, pt: (b, 0, 0)),
        scratch_shapes=[pltpu.VMEM((rows, 1), F32), pltpu.VMEM((rows, 1), F32),
                        pltpu.VMEM((rows, HEAD_V), F32), pltpu.VMEM((rows, HEAD_V), F32)],
    )
    out = pl.pallas_call(
        kernel,
        grid_spec=grid_spec,
        out_shape=jax.ShapeDtypeStruct((db, N_HEADS_A, HEAD_V), F32),
        compiler_params=_params(("parallel", "arbitrary")),
        name="decode_attn",
    )(page_table.reshape(-1), q8, k8, v8, lam_p, og, *([cache_k] * pps), *([cache_v] * pps))
    return out.reshape(db, D_A)


def _conv_tail(conv, lng, lnb, o, x, w_ref, hp=False):
    y = _layer_norm(conv, lng, lnb)
    c = y * jax.nn.sigmoid(y)
    return (x + _mm(o, w_ref[0:D_A, :], hp) + _mm(c, w_ref[D_A:D_A + D_B, :], hp))


def _even_out_kernel(hb_ref, hprev_ref, o_ref, x_ref, wdw_ref, bdw_ref, lng_ref, lnb_ref,
                     w_ref, y_ref, win_ref, conv_ref, *, tm, tiles_per_seq):
    i = pl.program_id(0)
    first = (i % tiles_per_seq) == 0
    win_ref[0, 0:CONV_HALO, :] = jnp.where(first, 0.0, hprev_ref[...])
    win_ref[0, CONV_HALO:CONV_HALO + tm, :] = hb_ref[...]
    lead = CONV_HALO - (CONV_WIDTH - 1)
    shifted_rows = tm + CONV_HALO - SUBLANES
    for s in range(1, SUBLANES):
        win_ref[s, 0:shifted_rows, :] = win_ref[0, s:s + shifted_rows, :]
    for r0 in range(0, tm, CONV_ROWS):
        for c0 in range(0, D_B, LANES):
            cols = slice(c0, c0 + LANES)
            acc = jnp.broadcast_to(bdw_ref[:, cols], (CONV_ROWS, LANES))
            for j in range(CONV_WIDTH):
                s = (lead + j) % SUBLANES
                a = r0 + lead + j - s
                acc = acc + win_ref[s, a:a + CONV_ROWS, cols] * wdw_ref[j:j + 1, cols]
            conv_ref[r0:r0 + CONV_ROWS, cols] = acc
    y_ref[...] = _conv_tail(conv_ref[...], lng_ref[...], lnb_ref[...], o_ref[...], x_ref[...],
                            w_ref)


def _even_out(hb, o, x, wdw, bdw, lng, lnb, w_out, tm, seq):
    T = x.shape[0]
    tiles_per_seq = seq // tm
    halo_blocks = tm // CONV_HALO
    row = lambda i: (i, 0)
    fix = lambda i: (0, 0)
    kernel = functools.partial(_even_out_kernel, tm=tm, tiles_per_seq=tiles_per_seq)
    return pl.pallas_call(
        kernel,
        grid=(T // tm,),
        in_specs=[pl.BlockSpec((tm, D_B), row),
                  pl.BlockSpec((CONV_HALO, D_B), lambda i: (jnp.maximum(i * halo_blocks - 1, 0), 0)),
                  pl.BlockSpec((tm, D_A), row),
                  pl.BlockSpec((tm, D_MODEL), row),
                  pl.BlockSpec((CONV_HALO, D_B), fix),
                  pl.BlockSpec((1, D_B), fix),
                  pl.BlockSpec((1, D_B), fix),
                  pl.BlockSpec((1, D_B), fix),
                  pl.BlockSpec((D_A + D_B, D_MODEL), fix)],
        out_specs=pl.BlockSpec((tm, D_MODEL), row),
        out_shape=jax.ShapeDtypeStruct((T, D_MODEL), F32),
        scratch_shapes=[pltpu.VMEM((SUBLANES, CONV_HALO + tm, D_B), F32),
                        pltpu.VMEM((tm, D_B), F32)],
        compiler_params=_params(("parallel",)),
        name="even_out",
    )(hb, hb, o, x, wdw, bdw, lng, lnb, w_out)


def _even_out_sample_kernel(st_ref, hb_ref, o_ref, x_ref, wdw_ref, bdw_ref, lng_ref, lnb_ref,
                            w_ref, y_ref, *, hp):
    w = wdw_ref[...]
    conv = jnp.sum(st_ref[...] * w[0:CONV_WIDTH - 1, :][None], axis=1)
    conv = conv + hb_ref[...] * w[CONV_WIDTH - 1:CONV_WIDTH, :] + bdw_ref[...]
    y_ref[...] = _conv_tail(conv, lng_ref[...], lnb_ref[...], o_ref[...], x_ref[...], w_ref,
                            hp)


def _even_out_sample(state, hb, o, x, wdw, bdw, lng, lnb, w_out, hp):
    db = x.shape[0]
    return pl.pallas_call(
        functools.partial(_even_out_sample_kernel, hp=hp),
        out_shape=jax.ShapeDtypeStruct((db, D_MODEL), F32),
        compiler_params=pltpu.CompilerParams(vmem_limit_bytes=VMEM_LIMIT),
        name="even_out_sample",
    )(state, hb, o, x, wdw, bdw, lng, lnb, w_out)


def _odd_front(x, g_ref, wuv_ref, lng_ref, lnb_ref, hp=False):
    h = _rms(x, g_ref[...])
    z = _mm(h, wuv_ref[...], hp)
    z = 0.5 * z * (1.0 + lax.erf(z * (1.0 / math.sqrt(2.0))))
    u = z[:, 0:D_C]
    v = _layer_norm(z[:, D_C:2 * D_C], lng_ref[...], lnb_ref[...])
    return u, v


def _odd_kernel(x_ref, g_ref, wuv_ref, lng_ref, lnb_ref, ws_ref, bs_ref, wo_ref, y_ref,
                sv_ref, *, tm):
    r = lax.broadcasted_iota(jnp.int32, (CHUNK, CHUNK), 0)
    c = lax.broadcasted_iota(jnp.int32, (CHUNK, CHUNK), 1)
    gw = D_C // N_GROUPS_C
    wsg = [jnp.where(c <= r, ws_ref[g], 0.0).astype(BF16)
           for g in range(N_GROUPS_C)]
    slab = tm // ODD_SPLIT
    for part in range(ODD_SPLIT):
        sl = slice(part * slab, (part + 1) * slab)
        x = x_ref[sl, :]
        u, v = _odd_front(x, g_ref, wuv_ref, lng_ref, lnb_ref)
        vb = v.astype(BF16)
        for g in range(N_GROUPS_C):
            cols = slice(g * gw, (g + 1) * gw)
            for ch in range(slab // CHUNK):
                rows = slice(ch * CHUNK, (ch + 1) * CHUNK)
                dst = slice(part * slab + ch * CHUNK, part * slab + (ch + 1) * CHUNK)
                sv_ref[dst, cols] = _dot(wsg[g], vb[rows, cols]) + bs_ref[:, cols]
        y_ref[sl, :] = x + _dot((u * sv_ref[sl, :]).astype(BF16), wo_ref[...])


def _odd(x, g, wuv, lng, lnb, ws, bs, wo, tm):
    T = x.shape[0]
    row = lambda i: (i, 0)
    fix = lambda i: (0, 0)
    return pl.pallas_call(
        functools.partial(_odd_kernel, tm=tm),
        grid=(T // tm,),
        in_specs=[pl.BlockSpec((tm, D_MODEL), row),
                  pl.BlockSpec((1, D_MODEL), fix),
                  pl.BlockSpec((D_MODEL, 2 * D_C), fix),
                  pl.BlockSpec((1, D_C), fix),
                  pl.BlockSpec((1, D_C), fix),
                  pl.BlockSpec((N_GROUPS_C, CHUNK, CHUNK), lambda i: (0, 0, 0)),
                  pl.BlockSpec((CHUNK, D_C), fix),
                  pl.BlockSpec((D_C, D_MODEL), fix)],
        out_specs=pl.BlockSpec((tm, D_MODEL), row),
        out_shape=jax.ShapeDtypeStruct((T, D_MODEL), F32),
        scratch_shapes=[pltpu.VMEM((tm, D_C), F32)],
        compiler_params=_params(("parallel",)),
        name="odd_mixer",
    )(x, g, wuv, lng, lnb, ws, bs, wo)


def _odd_sample_kernel(x_ref, g_ref, wuv_ref, lng_ref, lnb_ref, ws0_ref, bs0_ref, wo_ref,
                       y_ref, v_ref, *, hp):
    u, v = _odd_front(x_ref[...], g_ref, wuv_ref, lng_ref, lnb_ref, hp)
    v_ref[...] = v
    sv = v * ws0_ref[...] + bs0_ref[...]
    y_ref[...] = x_ref[...] + _mm(u * sv, wo_ref[...], hp)


def _odd_sample(x, g, wuv, lng, lnb, ws0, bs0, wo, hp):
    db = x.shape[0]
    return pl.pallas_call(
        functools.partial(_odd_sample_kernel, hp=hp),
        out_shape=[jax.ShapeDtypeStruct((db, D_MODEL), F32),
                   jax.ShapeDtypeStruct((db, D_C), F32)],
        compiler_params=pltpu.CompilerParams(vmem_limit_bytes=VMEM_LIMIT),
        name="odd_mixer_sample",
    )(x, g, wuv, lng, lnb, ws0, bs0, wo)


def _route(logits):
    lane = lax.broadcasted_iota(jnp.int32, logits.shape, 1)
    lane_f = lane.astype(F32)
    none = float(ROUTER_LANES)
    is_g = lane < N_EXPERT_GROUPS
    gl = jnp.where(is_g, logits, NEG_BIG)
    gmax = jnp.max(gl, axis=1, keepdims=True)
    gidx = jnp.min(jnp.where(is_g & (gl == gmax), lane_f, none), axis=1, keepdims=True)
    gw = 1.0 / jnp.sum(jnp.where(is_g, jnp.exp(gl - gmax), 0.0), axis=1, keepdims=True)
    e_f = lane_f - float(N_EXPERT_GROUPS)
    is_e = (lane >= N_EXPERT_GROUPS) & (lane < N_EXPERT_GROUPS + N_EXPERTS)
    grp_lo = gidx * float(EXPERTS_PER_GROUP)
    in_grp = is_e & (e_f >= grp_lo) & (e_f < grp_lo + float(EXPERTS_PER_GROUP))
    cand = jnp.where(in_grp, logits, NEG_BIG)
    v1 = jnp.max(cand, axis=1, keepdims=True)
    i1 = jnp.min(jnp.where(in_grp & (cand == v1), lane_f, none), axis=1, keepdims=True)
    rest = in_grp & (lane_f != i1)
    cand2 = jnp.where(rest, logits, NEG_BIG)
    v2 = jnp.max(cand2, axis=1, keepdims=True)
    i2 = jnp.min(jnp.where(rest & (cand2 == v2), lane_f, none), axis=1, keepdims=True)
    t = jnp.exp(v2 - v1)
    p1 = 1.0 / (1.0 + t)
    p2 = t * p1
    return jnp.where(lane_f == i1, p1 * gw, jnp.where(lane_f == i2, p2 * gw, 0.0))


def _moe_kernel(x_ref, g_ref, wr_ref, br_ref, wup_ref, wdn_ref, y_ref, xn_ref, gate_ref,
                acc_ref, *, hp):
    grp = pl.program_id(1)

    @pl.when(grp == 0)
    def _():
        xn = _rms(x_ref[...], g_ref[...]).astype(xn_ref.dtype)
        xn_ref[...] = xn
        gate_ref[...] = _route(_mm(xn, wr_ref[...], hp) + br_ref[...])
        acc_ref[...] = jnp.zeros(acc_ref.shape, F32)

    xn = xn_ref[...]
    gate = gate_ref[...]
    lane = lax.broadcasted_iota(jnp.int32, gate.shape, 1)
    acc = acc_ref[...]
    for j in range(EXPERTS_PER_GROUP):
        e_lane = N_EXPERT_GROUPS + grp * EXPERTS_PER_GROUP + j
        ge = jnp.sum(jnp.where(lane == e_lane, gate, 0.0), axis=1, keepdims=True)
        hu = _mm(xn, wup_ref[j], hp)
        a = hu[:, 0:D_FF_EXPERT]
        b = hu[:, D_FF_EXPERT:]
        act = a * jax.nn.sigmoid(a) * b * ge
        acc = acc + _mm(act, wdn_ref[j], hp)
    acc_ref[...] = acc

    @pl.when(grp == pl.num_programs(1) - 1)
    def _():
        y_ref[...] = x_ref[...] + acc_ref[...]


def _moe(x, g, wr, br, wup, wdn, layer, tm, hp=False):
    T = x.shape[0]
    row = lambda i, e: (i, 0)
    fix = lambda i, e: (0, 0)
    return pl.pallas_call(
        functools.partial(_moe_kernel, hp=hp),
        grid=(T // tm, N_EXPERT_GROUPS),
        in_specs=[pl.BlockSpec((tm, D_MODEL), row),
                  pl.BlockSpec((1, D_MODEL), fix),
                  pl.BlockSpec((D_MODEL, ROUTER_LANES), fix),
                  pl.BlockSpec((1, ROUTER_LANES), fix),
                  pl.BlockSpec((None, EXPERTS_PER_GROUP, D_MODEL, 2 * D_FF_EXPERT),
                               lambda i, e: (layer, e, 0, 0)),
                  pl.BlockSpec((None, EXPERTS_PER_GROUP, D_FF_EXPERT, D_MODEL),
                               lambda i, e: (layer, e, 0, 0))],
        out_specs=pl.BlockSpec((tm, D_MODEL), row),
        out_shape=jax.ShapeDtypeStruct((T, D_MODEL), F32),
        scratch_shapes=[pltpu.VMEM((tm, D_MODEL), F32 if hp else BF16),
                        pltpu.VMEM((tm, ROUTER_LANES), F32),
                        pltpu.VMEM((tm, D_MODEL), F32)],
        compiler_params=_params(("parallel", "arbitrary"), MOE_VMEM_LIMIT),
        name="hier_moe",
    )(x, g, wr, br, wup, wdn)


def _rope_tables(pos):
    half = HEAD_DIM // 2
    inv = ROPE_THETA ** (-jnp.arange(half, dtype=F32) / half)
    ang = pos.astype(F32)[:, None] * inv[None, :]
    cos, sin = jnp.cos(ang), jnp.sin(ang)
    reps = LANES // HEAD_DIM
    return (jnp.concatenate([cos, cos] * reps, axis=1),
            jnp.concatenate([-sin, sin] * reps, axis=1))


def _tile_gain(g, reps):
    return jnp.tile(g.astype(F32), reps)[None, :]


def kernel(x_prompt, x_sample, cache_k, cache_v, state_conv, page_table, norm_mix, norm_ffn, w_in_even, w_out_even, q_norm, k_norm, lambda_qk, head_norm, w_dw, b_dw, conv_ln_g, conv_ln_b, w_uv, v_ln_g, v_ln_b, w_spatial, b_spatial, w_out_odd, w_router_group, b_router_group, w_router_expert, b_router_expert, w_expert_up, w_expert_down):
    batch, seq, _ = x_prompt.shape
    db = x_sample.shape[0]
    n_pages = page_table.shape[1]
    page = cache_k.shape[2]
    past_len = n_pages * page
    tm = TILE_TOKENS

    xp = x_prompt.reshape(batch * seq, D_MODEL)
    xs = x_sample.reshape(db, D_MODEL)
    cos_p, sin_p = _rope_tables(jnp.arange(seq))
    cos_s, sin_s = _rope_tables(jnp.full((db,), past_len))
    ck = cache_k.reshape(cache_k.shape[0], cache_k.shape[1], page * N_HEADS_A, HEAD_V)
    cv = cache_v.reshape(cache_v.shape[0], cache_v.shape[1], page * N_HEADS_A, HEAD_V)
    group_of = jnp.arange(D_A) // HEAD_DIM
    gmat32 = jnp.where(group_of[:, None] == group_of[None, :], 1.0 / HEAD_DIM, 0.0).astype(F32)
    gmat = gmat32.astype(BF16)
    w_in_bf = w_in_even.astype(BF16)

    cp_l, cs_l, chs_l = [], [], []
    n_even = w_in_even.shape[0]
    kv_p = kv_s = None
    for layer in range(DEPTH):
        i = layer // 2
        g_mix = norm_mix[layer][None, :]
        hp = layer < HP_SAMPLE_LAYERS
        if layer % 2 == 0:
            lam_init = 0.8 - 0.6 * math.exp(-0.3 * layer)
            w_out = w_out_even[i].astype(BF16)
            qg = _tile_gain(q_norm[i], D_A // HEAD_DIM)
            kg = _tile_gain(k_norm[i], D_A // HEAD_DIM)
            og = head_norm[i][None, :]
            wdw = jnp.pad(w_dw[i], ((0, CONV_HALO - CONV_WIDTH), (0, 0)))
            bdw, lng, lnb = b_dw[i][None, :], conv_ln_g[i][None, :], conv_ln_b[i][None, :]
            q, kp_all, kb, vp_all, vb, hb = _even_in(xp, g_mix, w_in_bf, i, n_even, kv_p, qg, kg,
                                                     gmat, cos_p, sin_p, tm, seq // tm)
            kv_p = (kp_all, vp_all)
            o = _prompt_attention(q, kb, vb, lambda_qk[i], og, batch, seq, lam_init)
            xp = _even_out(hb, o, xp, wdw, bdw, lng, lnb, w_out, CONV_TILE_TOKENS, seq)
            cp_l.append(hb.reshape(batch, seq, D_B)[:, seq - (CONV_WIDTH - 1):, :])
            qs, ks_all, _, vs_all, _, hbs = _even_in(
                xs, g_mix, w_in_even if hp else w_in_bf, i, n_even, kv_s, qg, kg,
                gmat32 if hp else gmat, cos_s, sin_s, db, 1, hp)
            kv_s = (ks_all, vs_all)
            os_ = _decode_attention(page_table, qs, ks_all[i], vs_all[i], lambda_qk[i], og, ck, cv,
                                    i, lam_init, hp)
            xs = _even_out_sample(state_conv[i], hbs, os_, xs, wdw, bdw, lng,
                                  lnb, w_out_even[i] if hp else w_out, hp)
            cs_l.append(jnp.concatenate([state_conv[i][:, 1:, :], hbs[:, None, :]], axis=1))
        else:
            wuv = w_uv[i].astype(BF16)
            wo = w_out_odd[i].astype(BF16)
            lng, lnb = v_ln_g[i][None, :], v_ln_b[i][None, :]
            gw = D_C // N_GROUPS_C
            bs = jnp.repeat(b_spatial[i].T, gw, axis=1)
            xp = _odd(xp, g_mix, wuv, lng, lnb, w_spatial[i], bs, wo, tm)
            ws0 = jnp.repeat(w_spatial[i][:, 0, 0], gw)[None, :]
            xs, vch = _odd_sample(xs, g_mix, w_uv[i] if hp else wuv, lng, lnb, ws0, bs[0:1, :],
                                  w_out_odd[i] if hp else wo, hp)
            chs_l.append(vch.reshape(db, 1, D_C))
        g_ffn = norm_ffn[layer][None, :]
        pad = ROUTER_LANES - N_EXPERT_GROUPS - N_EXPERTS
        wr32 = jnp.pad(jnp.concatenate([w_router_group[layer], w_router_expert[layer]], axis=1),
                       ((0, 0), (0, pad)))
        wr = wr32.astype(BF16)
        br = jnp.pad(jnp.concatenate([b_router_group[layer], b_router_expert[layer]]),
                     (0, pad))[None, :]
        xp = _moe(xp, g_ffn, wr, br, w_expert_up, w_expert_down, layer, MOE_TILE_TOKENS)
        xs = _moe(xs, g_ffn, wr32 if hp else wr, br, w_expert_up, w_expert_down, layer, db, hp)

    return (xp.reshape(batch, seq, D_MODEL), xs.reshape(db, 1, D_MODEL),
            kv_p[0].reshape(n_even, batch, seq, N_HEADS_A, HEAD_V),
            kv_p[1].reshape(n_even, batch, seq, N_HEADS_A, HEAD_V),
            kv_s[0].reshape(n_even, db, 1, N_HEADS_A, HEAD_V),
            kv_s[1].reshape(n_even, db, 1, N_HEADS_A, HEAD_V),
            jnp.stack(cp_l), jnp.stack(cs_l), jnp.stack(chs_l))
```
